```python
import jax, jax.numpy as jnp
from jax import lax
import numpy as np

D_MODEL = 1024
BATCH = 2
SEQ = 8192
DEPTH = 1

PLE_DIM = 256
D_CONV = 512
CONV_A_WIDTH = 3
N_HEADS = 4
HEAD_K = 128
HEAD_V = 128
D_DN_K = N_HEADS * HEAD_K
D_DN_V = N_HEADS * HEAD_V
CONV_DN_WIDTH = 3
CHUNK = 64
D_FF = 4 * D_MODEL
EPS = 1e-6

SPLITS = (D_CONV, D_CONV, D_CONV, 2 * D_DN_K + D_DN_V, D_DN_V,
          N_HEADS, N_HEADS, N_HEADS, N_HEADS, D_MODEL, D_MODEL)
D_IN_PROJ = sum(SPLITS)

kernel_name = "hybrid_gated_conv_deltanet_encoder_block"


def _split_points():
    pts, acc = [], 0
    for w in SPLITS[:-1]:
        acc += w
        pts.append(acc)
    return pts


def rms_norm(x, g):
    xf = x.astype(jnp.float32)
    y = xf * lax.rsqrt(jnp.mean(xf * xf, axis=-1, keepdims=True) + EPS)
    return (y * g.astype(jnp.float32)).astype(x.dtype)


def l2_norm(x):
    return x * lax.rsqrt(jnp.sum(x * x, axis=-1, keepdims=True) + EPS)


def dwconv_centred(x, w):
    k = w.shape[0]
    r = k // 2
    s = x.shape[1]
    xp = jnp.pad(x, ((0, 0), (r, r), (0, 0)))
    return sum(xp[:, j:j + s, :] * w[j] for j in range(k))


def gated_delta_rule(q, k, v, g, beta):
    b, s, h, dk = q.shape
    dv = v.shape[-1]
    n = s // CHUNK

    def chunks(t):
        return t.reshape(b, n, CHUNK, h, -1).transpose(0, 3, 1, 2, 4)

    q, k, v = chunks(q), chunks(k), chunks(v)
    g = jnp.cumsum(g.reshape(b, n, CHUNK, h).transpose(0, 3, 1, 2), axis=-1)
    beta = beta.reshape(b, n, CHUNK, h).transpose(0, 3, 1, 2)

    tri = jnp.tril(jnp.ones((CHUNK, CHUNK), dtype=bool))
    strict = jnp.tril(jnp.ones((CHUNK, CHUNK), dtype=bool), -1)
    decay = jnp.exp(jnp.where(tri, g[..., :, None] - g[..., None, :], -jnp.inf))

    kb = k * beta[..., None]
    vb = v * beta[..., None]
    a = jnp.where(strict, jnp.einsum('bhncd,bhnjd->bhncj', kb, k) * decay, 0.0)
    eye = jnp.eye(CHUNK, dtype=jnp.float32)
    t_inv = lax.linalg.triangular_solve(eye + a, jnp.broadcast_to(eye, a.shape),
                                        left_side=True, lower=True, unit_diagonal=True)
    u = t_inv @ vb
    w = t_inv @ (kb * jnp.exp(g)[..., None])
    intra = jnp.einsum('bhncd,bhnjd->bhncj', q, k) * decay

    def step(state, inp):
        q_c, k_c, u_c, w_c, g_c, a_c = inp
        v_new = u_c - w_c @ state
        o = (q_c * jnp.exp(g_c)[..., None]) @ state + a_c @ v_new
        g_last = g_c[..., -1]
        state = state * jnp.exp(g_last)[..., None, None] + jnp.einsum(
            'bhcd,bhce->bhde', k_c * jnp.exp(g_last[..., None] - g_c)[..., None], v_new)
        return state, o

    xs = tuple(jnp.moveaxis(t, 2, 0) for t in (q, k, u, w, g, intra))
    state0 = jnp.zeros((b, h, dk, dv), jnp.float32)
    _, o = lax.scan(step, state0, xs)
    return o.transpose(1, 0, 3, 2, 4).reshape(b, s, h, dv)


def setup_inputs(seed: int = 0) -> dict:
    key = jax.random.key(seed)
    ks = jax.random.split(key, 24)
    L = DEPTH
    f32 = jnp.float32

    def dense(k, fan_in, fan_out):
        return jax.random.normal(k, (L, fan_in, fan_out), f32) * fan_in ** -0.5

    def gain(k, n):
        return 1.0 + 0.02 * jax.random.normal(k, (L, n), f32)

    a_log_f = jnp.log(jax.random.uniform(ks[5], (L, N_HEADS), f32, 1.0, 16.0))
    a_log_b = jnp.log(jax.random.uniform(ks[6], (L, N_HEADS), f32, 1.0, 16.0))

    def dt_bias(k):
        u = jax.random.uniform(k, (L, N_HEADS), f32)
        dt = jnp.exp(u * (jnp.log(0.1) - jnp.log(0.001)) + jnp.log(0.001))
        return dt + jnp.log(-jnp.expm1(-dt))

    return {
        "x": jax.random.normal(ks[0], (BATCH, SEQ, D_MODEL), f32),
        "p": jax.random.normal(ks[1], (DEPTH, BATCH, SEQ, PLE_DIM), f32),
        "g_mix": gain(ks[2], D_MODEL),
        "w_in": dense(ks[3], D_MODEL, D_IN_PROJ),
        "conv_a": jax.random.normal(ks[4], (L, CONV_A_WIDTH, D_CONV), f32) * CONV_A_WIDTH ** -0.5,
        "conv_qkv": jax.random.normal(ks[7], (L, CONV_DN_WIDTH, 2 * D_DN_K + D_DN_V), f32) * CONV_DN_WIDTH ** -0.5,
        "a_log_f": a_log_f,
        "a_log_b": a_log_b,
        "dt_bias_f": dt_bias(ks[8]),
        "dt_bias_b": dt_bias(ks[9]),
        "g_dn_norm": gain(ks[10], HEAD_V),
        "w_a_out": dense(ks[11], D_CONV, D_MODEL),
        "w_dn_out": dense(ks[12], D_DN_V, D_MODEL),
        "w_o": dense(ks[13], D_MODEL, D_MODEL),
        "g_ffn": gain(ks[14], D_MODEL),
        "w_ff1": dense(ks[15], D_MODEL, D_FF),
        "w_ff2": dense(ks[16], D_FF, D_MODEL),
        "g_ple": gain(ks[17], D_MODEL),
        "w_ple_gate": dense(ks[18], D_MODEL, D_MODEL),
        "w_ple_proj": dense(ks[19], PLE_DIM, D_MODEL),
        "g_final": 1.0 + 0.02 * jax.random.normal(ks[20], (D_MODEL,), f32),
    }


def reference(x, p, g_mix, w_in, conv_a, conv_qkv, a_log_f, a_log_b, dt_bias_f, dt_bias_b,
              g_dn_norm, w_a_out, w_dn_out, w_o, g_ffn, w_ff1, w_ff2, g_ple, w_ple_gate,
              w_ple_proj, g_final):
    b, s, _ = x.shape
    f32 = jnp.float32
    h = x
    for i in range(DEPTH):
        n = rms_norm(h, g_mix[i])
        proj = n @ w_in[i]
        (b_a, c_a, x_a, qkv, z, beta_f, beta_b, alpha_f, alpha_b,
         gate_a, gate_dn) = jnp.split(proj, _split_points(), axis=-1)

        y_a = c_a * dwconv_centred(b_a * x_a, conv_a[i])

        qkv = jax.nn.silu(dwconv_centred(qkv, conv_qkv[i]))
        q, k, v = jnp.split(qkv, [D_DN_K, 2 * D_DN_K], axis=-1)
        q = l2_norm(q.reshape(b, s, N_HEADS, HEAD_K).astype(f32)) * HEAD_K ** -0.5
        k = l2_norm(k.reshape(b, s, N_HEADS, HEAD_K).astype(f32))
        v = v.reshape(b, s, N_HEADS, HEAD_V).astype(f32)
        g_f = -jnp.exp(a_log_f[i].astype(f32)) * jax.nn.softplus(alpha_f.astype(f32) + dt_bias_f[i].astype(f32))
        g_b = -jnp.exp(a_log_b[i].astype(f32)) * jax.nn.softplus(alpha_b.astype(f32) + dt_bias_b[i].astype(f32))
        bt_f = jax.nn.sigmoid(beta_f.astype(f32))
        bt_b = jax.nn.sigmoid(beta_b.astype(f32))
        flip = lambda t: jnp.flip(t, axis=1)
        o = gated_delta_rule(q, k, v, g_f, bt_f) + flip(
            gated_delta_rule(flip(q), flip(k), flip(v), flip(g_b), flip(bt_b)))
        o = rms_norm(o, g_dn_norm[i]) * jax.nn.silu(z.reshape(b, s, N_HEADS, HEAD_V).astype(f32))
        y_dn = o.reshape(b, s, D_DN_V).astype(h.dtype)

        merged = (jax.nn.sigmoid(gate_a) * (y_a @ w_a_out[i])
                  + jax.nn.sigmoid(gate_dn) * (y_dn @ w_dn_out[i]))
        h = h + merged @ w_o[i]

        n2 = rms_norm(h, g_ffn[i])
        h = h + jnp.square(jax.nn.relu(n2 @ w_ff1[i])) @ w_ff2[i]

        gate_p = jax.nn.sigmoid(rms_norm(h, g_ple[i]) @ w_ple_gate[i])
        h = h + gate_p * (p[i] @ w_ple_proj[i])
    return rms_norm(h, g_final)
```

```python
import functools

import jax
import jax.numpy as jnp
from jax import lax
from jax.experimental import pallas as pl
from jax.experimental.pallas import tpu as pltpu

F32 = jnp.float32
BF16 = jnp.bfloat16

D_MODEL = 1024
PLE_DIM = 256
D_CONV = 512
N_HEADS = 4
HEAD_DIM = 128
D_DN = N_HEADS * HEAD_DIM
CHUNK = 64
D_FF = 4 * D_MODEL
EPS = 1e-6

LANES = 128
VMEM_LIMIT = 56 * 1024 * 1024

TM_PROJ = 512
TS_PREP = 512
TC_DELTA = 512
TM_OUT = 512
FF_CHUNK = 1024

CS_BETA, CS_G, CS_EG, CS_EKG, CS_EGT = 0, 8, 16, 24, 32


def _resident(shape):
    nd = len(shape)
    return pl.BlockSpec(shape, lambda *_: (0,) * nd, pipeline_mode=pl.Buffered(1))


def _rms(x, g):
    return x * lax.rsqrt(jnp.mean(x * x, axis=-1, keepdims=True) + EPS) * g


def _dot(a, b):
    return jnp.dot(a, b, preferred_element_type=F32)


def _silu(x):
    return x * jax.nn.sigmoid(x)


def _proj_kernel(x_ref, g_ref, wpa_ref, wqkv_ref, wz_ref, wsm_ref, wg_ref,
                 pa_ref, qkv_ref, z_ref, sm_ref, gates_ref):
    n = _rms(x_ref[...], g_ref[...]).astype(BF16)
    pa_ref[...] = _dot(n, wpa_ref[...])
    qkv_ref[...] = _dot(n, wqkv_ref[...])
    z_ref[...] = _dot(n, wz_ref[...]).astype(z_ref.dtype)
    sm_ref[...] = _dot(n, wsm_ref[...])
    gates_ref[...] = _dot(n, wg_ref[...]).astype(gates_ref.dtype)


def _proj(x2, g_mix, wpa, wqkv, wz, wsm, wg):
    t = x2.shape[0]
    tm = TM_PROJ
    row = lambda w: pl.BlockSpec((tm, w), lambda i: (i, 0))
    return pl.pallas_call(
        _proj_kernel,
        grid=(t // tm,),
        in_specs=[row(D_MODEL), _resident(g_mix.shape), _resident(wpa.shape), _resident(wqkv.shape),
                  _resident(wz.shape), _resident(wsm.shape), _resident(wg.shape)],
        out_specs=[row(3 * D_CONV), row(3 * D_DN), row(D_DN), row(LANES), row(2 * D_MODEL)],
        out_shape=[jax.ShapeDtypeStruct((t, 3 * D_CONV), F32),
                   jax.ShapeDtypeStruct((t, 3 * D_DN), F32),
                   jax.ShapeDtypeStruct((t, D_DN), BF16),
                   jax.ShapeDtypeStruct((t, LANES), F32),
                   jax.ShapeDtypeStruct((t, 2 * D_MODEL), BF16)],
        compiler_params=pltpu.CompilerParams(dimension_semantics=("arbitrary",),
                                             vmem_limit_bytes=VMEM_LIMIT),
        name="proj",
    )(x2, g_mix, wpa, wqkv, wz, wsm, wg)


def _conv3(cur, prev_row, next_row, w):
    ts = cur.shape[0]
    rid = lax.broadcasted_iota(jnp.int32, cur.shape, 0)
    up = jnp.where(rid == 0, prev_row, pltpu.roll(cur, 1, 0))
    dn = jnp.where(rid == ts - 1, next_row, pltpu.roll(cur, ts - 1, 0))
    return up * w[0:1, :] + cur * w[1:2, :] + dn * w[2:3, :]


def _split3(x):
    hi = x.astype(BF16)
    r1 = x - hi.astype(F32)
    mid = r1.astype(BF16)
    lo = (r1 - mid.astype(F32)).astype(BF16)
    return hi, mid, lo


def _prep_kernel(pa_ref, pa_prev_ref, pa_next_ref, qkv_ref, qkv_prev_ref, qkv_next_ref, sm_ref,
                 conv_a_ref, conv_qkv_ref, alog_ref, dtb_ref,
                 ya_ref, q_ref, k_ref, v_ref, cs_ref, gt_ref):
    j = pl.program_id(1)
    nj = pl.num_programs(1)
    ts = TS_PREP
    has_prev = (j > 0).astype(F32)
    has_next = (j < nj - 1).astype(F32)

    pa = pa_ref[0]
    pa_p = pa_prev_ref[0][7:8, :] * has_prev
    pa_n = pa_next_ref[0][0:1, :] * has_next
    u = pa[:, :D_CONV] * pa[:, 2 * D_CONV:]
    u_p = pa_p[:, :D_CONV] * pa_p[:, 2 * D_CONV:]
    u_n = pa_n[:, :D_CONV] * pa_n[:, 2 * D_CONV:]
    ya = pa[:, D_CONV:2 * D_CONV] * _conv3(u, u_p, u_n, conv_a_ref[...])
    ya_ref[0] = ya.astype(ya_ref.dtype)

    qkv = qkv_ref[0]
    qkv_p = qkv_prev_ref[0][7:8, :] * has_prev
    qkv_n = qkv_next_ref[0][0:1, :] * has_next
    c = _silu(_conv3(qkv, qkv_p, qkv_n, conv_qkv_ref[...]))
    for h in range(N_HEADS):
        lo = h * HEAD_DIM
        qh = c[:, lo:lo + HEAD_DIM]
        kh = c[:, D_DN + lo:D_DN + lo + HEAD_DIM]
        qn = qh * lax.rsqrt(jnp.sum(qh * qh, axis=-1, keepdims=True) + EPS) * (HEAD_DIM ** -0.5)
        kn = kh * lax.rsqrt(jnp.sum(kh * kh, axis=-1, keepdims=True) + EPS)
        q_ref[0, :, lo:lo + HEAD_DIM] = qn
        k_ref[0, :, lo:lo + HEAD_DIM] = kn
    v_ref[0] = c[:, 2 * D_DN:]

    sm = sm_ref[0]
    lane = lax.broadcasted_iota(jnp.int32, sm.shape, 1)
    beta = jax.nn.sigmoid(sm)
    xs = sm + dtb_ref[...]
    softplus = jnp.maximum(xs, 0.0) + jnp.log1p(jnp.exp(-jnp.abs(xs)))
    g = jnp.where((lane >= 8) & (lane < 16), -jnp.exp(alog_ref[...]) * softplus, 0.0)

    r = lax.broadcasted_iota(jnp.int32, (ts, ts), 0)
    cc = lax.broadcasted_iota(jnp.int32, (ts, ts), 1)
    same = (r // CHUNK) == (cc // CHUNK)
    m_tot = jnp.where(same, 1.0, 0.0).astype(BF16)
    m_pre = jnp.where(same & (cc <= r), 1.0, 0.0).astype(BF16)
    m_suf = jnp.where(same & (cc >= r), 1.0, 0.0).astype(BF16)
    parts = _split3(g)
    pre = sum(_dot(m_pre, p) for p in parts)
    suf = sum(_dot(m_suf, p) for p in parts)
    tot = sum(_dot(m_tot, p) for p in parts)
    gc = jnp.where(lane < 12, pre, suf)
    eg = jnp.exp(gc)
    ekg = jnp.exp(tot - gc)
    egt = jnp.exp(tot)

    g8 = pltpu.roll(gc, LANES - 8, 1)
    cs = jnp.where(lane < 8, beta, 0.0)
    cs = jnp.where((lane >= 8) & (lane < 16), gc, cs)
    cs = jnp.where((lane >= 16) & (lane < 24), pltpu.roll(eg, 8, 1), cs)
    cs = jnp.where((lane >= 24) & (lane < 32), pltpu.roll(ekg, 16, 1), cs)
    cs = jnp.where((lane >= 32) & (lane < 40), pltpu.roll(egt, 24, 1), cs)
    cs_ref[0] = cs

    g8t = g8.T
    for n in range(ts // CHUNK):
        gt_ref[0, n] = g8t[0:8, n * CHUNK:(n + 1) * CHUNK]


def _prep(pa, qkv, sm, conv_a, conv_qkv, alog, dtb):
    b, s, _ = pa.shape
    ts = TS_PREP
    nj = s // ts
    r8 = ts // 8
    n8 = s // 8

    def cur(w):
        return pl.BlockSpec((1, ts, w), lambda i, j: (i, j, 0))

    def prev(w):
        return pl.BlockSpec((1, 8, w), lambda i, j: (i, jnp.maximum(j * r8 - 1, 0), 0))

    def nxt(w):
        return pl.BlockSpec((1, 8, w), lambda i, j: (i, jnp.minimum((j + 1) * r8, n8 - 1), 0))

    return pl.pallas_call(
        _prep_kernel,
        grid=(b, nj),
        in_specs=[cur(3 * D_CONV), prev(3 * D_CONV), nxt(3 * D_CONV),
                  cur(3 * D_DN), prev(3 * D_DN), nxt(3 * D_DN), cur(LANES),
                  _resident(conv_a.shape), _resident(conv_qkv.shape),
                  _resident(alog.shape), _resident(dtb.shape)],
        out_specs=[cur(D_CONV), cur(D_DN), cur(D_DN), cur(D_DN), cur(LANES),
                   pl.BlockSpec((1, ts // CHUNK, 8, CHUNK), lambda i, j: (i, j, 0, 0))],
        out_shape=[jax.ShapeDtypeStruct((b, s, D_CONV), BF16),
                   jax.ShapeDtypeStruct((b, s, D_DN), F32),
                   jax.ShapeDtypeStruct((b, s, D_DN), F32),
                   jax.ShapeDtypeStruct((b, s, D_DN), F32),
                   jax.ShapeDtypeStruct((b, s, LANES), F32),
                   jax.ShapeDtypeStruct((b, s // CHUNK, 8, CHUNK), F32)],
        compiler_params=pltpu.CompilerParams(dimension_semantics=("arbitrary", "arbitrary"),
                                             vmem_limit_bytes=VMEM_LIMIT),
        name="prep",
    )(pa, pa, pa, qkv, qkv, qkv, sm, conv_a, conv_qkv, alog, dtb)


def _tri_inv(a):
    n = a.shape[0]
    r = lax.broadcasted_iota(jnp.int32, (n, n), 0)
    c = lax.broadcasted_iota(jnp.int32, (n, n), 1)
    p = jnp.where(r == c, 1.0, 0.0) - a
    m = a.astype(BF16)
    steps = 1
    while 2 * steps < n:
        steps *= 2
        m2 = _dot(m, m)
        m = m2.astype(BF16)
        p = p + _dot(p.astype(BF16), m)
    return p


def _delta_chain(q, k, v, csb, grow, col, s_ref, sidx, lower):
    beta = csb[:, CS_BETA + col:CS_BETA + col + 1]
    gcol = csb[:, CS_G + col:CS_G + col + 1]
    eg = csb[:, CS_EG + col:CS_EG + col + 1]
    ekg = csb[:, CS_EKG + col:CS_EKG + col + 1]
    egt = csb[0:1, CS_EGT + col:CS_EGT + col + 1]

    r = lax.broadcasted_iota(jnp.int32, (CHUNK, CHUNK), 0)
    c = lax.broadcasted_iota(jnp.int32, (CHUNK, CHUNK), 1)
    incl = (r >= c) if lower else (r <= c)
    strict = (r > c) if lower else (r < c)

    kb = k * beta
    vb = v * beta
    kbf = k.astype(BF16)
    lhs = jnp.concatenate([kb, q], axis=0).astype(BF16)
    kq = lax.dot_general(lhs, kbf, (((1,), (1,)), ((), ())), preferred_element_type=F32)
    decay = jnp.where(incl, jnp.exp(jnp.where(incl, gcol - grow, 0.0)), 0.0)
    a = jnp.where(strict, kq[:CHUNK] * decay, 0.0)
    intra = kq[CHUNK:] * decay
    t_inv = _tri_inv(a)

    rhs = jnp.concatenate([vb, kb * eg], axis=1).astype(BF16)
    uw = _dot(t_inv.astype(BF16), rhs)
    u = uw[:, :HEAD_DIM]
    w = uw[:, HEAD_DIM:]

    state = s_ref[sidx]
    lhs2 = jnp.concatenate([w, q * eg], axis=0).astype(BF16)
    ws = _dot(lhs2, state.astype(BF16))
    v_new = u - ws[:CHUNK]
    v_new_b = v_new.astype(BF16)
    o = ws[CHUNK:] + _dot(intra.astype(BF16), v_new_b)
    kg = (k * ekg).astype(BF16)
    upd = lax.dot_general(kg, v_new_b, (((0,), (0,)), ((), ())), preferred_element_type=F32)
    s_ref[sidx] = state * egt + upd
    return o


def _delta_kernel(qf_ref, kf_ref, vf_ref, csf_ref, gtf_ref, qb_ref, kb_ref, vb_ref, csb_ref, gtb_ref,
                  of_ref, ob_ref, s_ref):
    @pl.when(pl.program_id(1) == 0)
    def _():
        s_ref[...] = jnp.zeros_like(s_ref)

    nc = TC_DELTA // CHUNK

    def body(i, carry):
        rf = pl.multiple_of(i * CHUNK, CHUNK)
        ib = nc - 1 - i
        rb = pl.multiple_of(ib * CHUNK, CHUNK)
        csf = csf_ref[0, pl.ds(rf, CHUNK), :]
        csb = csb_ref[0, pl.ds(rb, CHUNK), :]
        gtf = gtf_ref[0, i]
        gtb = gtb_ref[0, ib]
        for h in range(N_HEADS):
            lo = h * HEAD_DIM
            sl = slice(lo, lo + HEAD_DIM)
            of_ref[0, pl.ds(rf, CHUNK), sl] = _delta_chain(
                qf_ref[0, pl.ds(rf, CHUNK), sl], kf_ref[0, pl.ds(rf, CHUNK), sl],
                vf_ref[0, pl.ds(rf, CHUNK), sl], csf, gtf[h:h + 1, :], h, s_ref, h, True)
            ob_ref[0, pl.ds(rb, CHUNK), sl] = _delta_chain(
                qb_ref[0, pl.ds(rb, CHUNK), sl], kb_ref[0, pl.ds(rb, CHUNK), sl],
                vb_ref[0, pl.ds(rb, CHUNK), sl], csb, gtb[4 + h:5 + h, :], 4 + h, s_ref, 4 + h, False)
        return carry

    lax.fori_loop(0, nc, body, 0)


def _delta(q, k, v, cs, gt):
    b, s, _ = q.shape
    tc = TC_DELTA
    nb = s // tc
    ncb = tc // CHUNK
    fwd = lambda w: pl.BlockSpec((1, tc, w), lambda i, j: (i, j, 0))
    bwd = lambda w: pl.BlockSpec((1, tc, w), lambda i, j: (i, nb - 1 - j, 0))
    gt_f = pl.BlockSpec((1, ncb, 8, CHUNK), lambda i, j: (i, j, 0, 0))
    gt_b = pl.BlockSpec((1, ncb, 8, CHUNK), lambda i, j: (i, nb - 1 - j, 0, 0))
    return pl.pallas_call(
        _delta_kernel,
        grid=(b, nb),
        in_specs=[fwd(D_DN), fwd(D_DN), fwd(D_DN), fwd(LANES), gt_f,
                  bwd(D_DN), bwd(D_DN), bwd(D_DN), bwd(LANES), gt_b],
        out_specs=[fwd(D_DN), bwd(D_DN)],
        out_shape=[jax.ShapeDtypeStruct((b, s, D_DN), F32), jax.ShapeDtypeStruct((b, s, D_DN), F32)],
        scratch_shapes=[pltpu.VMEM((2 * N_HEADS, HEAD_DIM, HEAD_DIM), F32)],
        compiler_params=pltpu.CompilerParams(dimension_semantics=("arbitrary", "arbitrary"),
                                             vmem_limit_bytes=VMEM_LIMIT),
        name="delta",
    )(q, k, v, cs, gt, q, k, v, cs, gt)


def _out_kernel(x_ref, p_ref, of_ref, ob_ref, z_ref, ya_ref, gates_ref,
                gdn_ref, wa_ref, wdn_ref, wo_ref, gffn_ref, w1_ref, w2_ref,
                gple_ref, wpg_ref, wpp_ref, gfin_ref, out_ref):
    o = of_ref[...] + ob_ref[...]
    z = z_ref[...].astype(F32)
    gdn = gdn_ref[...]
    parts = []
    for h in range(N_HEADS):
        sl = slice(h * HEAD_DIM, (h + 1) * HEAD_DIM)
        parts.append(_rms(o[:, sl], gdn) * _silu(z[:, sl]))
    y_dn = jnp.concatenate(parts, axis=1).astype(BF16)
    gates = gates_ref[...].astype(F32)
    merged = (jax.nn.sigmoid(gates[:, :D_MODEL]) * _dot(ya_ref[...], wa_ref[...])
              + jax.nn.sigmoid(gates[:, D_MODEL:]) * _dot(y_dn, wdn_ref[...]))
    h1 = x_ref[...] + _dot(merged.astype(BF16), wo_ref[...])

    n2 = _rms(h1, gffn_ref[...]).astype(BF16)
    h2 = h1
    for c in range(D_FF // FF_CHUNK):
        sl = slice(c * FF_CHUNK, (c + 1) * FF_CHUNK)
        a = jnp.maximum(_dot(n2, w1_ref[:, sl]), 0.0)
        h2 = h2 + _dot((a * a).astype(BF16), w2_ref[sl, :])

    n3 = _rms(h2, gple_ref[...]).astype(BF16)
    gate_p = jax.nn.sigmoid(_dot(n3, wpg_ref[...]))
    h3 = h2 + gate_p * _dot(p_ref[...].astype(BF16), wpp_ref[...])
    out_ref[...] = _rms(h3, gfin_ref[...])


def _out(x2, p2, o_f, o_b, z, ya, gates, gdn, wa, wdn, wo, gffn, w1, w2, gple, wpg, wpp, gfin):
    t = x2.shape[0]
    tm = TM_OUT
    row = lambda w: pl.BlockSpec((tm, w), lambda i: (i, 0))
    weights = (gdn, wa, wdn, wo, gffn, w1, w2, gple, wpg, wpp, gfin)
    return pl.pallas_call(
        _out_kernel,
        grid=(t // tm,),
        in_specs=[row(D_MODEL), row(PLE_DIM), row(D_DN), row(D_DN), row(D_DN), row(D_CONV),
                  row(2 * D_MODEL)] + [_resident(w.shape) for w in weights],
        out_specs=row(D_MODEL),
        out_shape=jax.ShapeDtypeStruct((t, D_MODEL), F32),
        compiler_params=pltpu.CompilerParams(dimension_semantics=("arbitrary",),
                                             vmem_limit_bytes=VMEM_LIMIT),
        name="out",
    )(x2, p2, o_f, o_b, z, ya, gates, *weights)


def kernel(x, p, g_mix, w_in, conv_a, conv_qkv, a_log_f, a_log_b, dt_bias_f, dt_bias_b, g_dn_norm,
           w_a_out, w_dn_out, w_o, g_ffn, w_ff1, w_ff2, g_ple, w_ple_gate, w_ple_proj, g_final):
    b, s, d = x.shape
    depth = p.shape[0]
    assert depth == 1 and d == D_MODEL and s % TC_DELTA == 0 and s % TS_PREP == 0
    t = b * s
    i = 0

    w = w_in[i]
    c0 = 3 * D_CONV
    c1 = c0 + 3 * D_DN
    c2 = c1 + D_DN
    c3 = c2 + 4 * N_HEADS
    wpa = w[:, :c0].astype(BF16)
    wqkv = w[:, c0:c1].astype(BF16)
    wz = w[:, c1:c2].astype(BF16)
    wsm = jnp.pad(w[:, c2:c3], ((0, 0), (0, LANES - 4 * N_HEADS))).astype(BF16)
    wg = w[:, c3:].astype(BF16)

    x2 = x.reshape(t, d)
    pa, qkv, z, sm, gates = _proj(x2, g_mix[i][None, :], wpa, wqkv, wz, wsm, wg)

    zeros8 = jnp.zeros((2 * N_HEADS,), F32)
    tail = jnp.zeros((LANES - 4 * N_HEADS,), F32)
    alog = jnp.concatenate([zeros8, a_log_f[i], a_log_b[i], tail])[None, :]
    dtb = jnp.concatenate([zeros8, dt_bias_f[i], dt_bias_b[i], tail])[None, :]
    ya, q, k, v, cs, gt = _prep(pa.reshape(b, s, -1), qkv.reshape(b, s, -1), sm.reshape(b, s, -1),
                                conv_a[i], conv_qkv[i], alog, dtb)

    o_f, o_b = _delta(q, k, v, cs, gt)

    out = _out(x2, p[i].reshape(t, PLE_DIM), o_f.reshape(t, D_DN), o_b.reshape(t, D_DN), z,
               ya.reshape(t, D_CONV), gates,
               g_dn_norm[i][None, :], w_a_out[i].astype(BF16), w_dn_out[i].astype(BF16),
               w_o[i].astype(BF16), g_ffn[i][None, :], w_ff1[i].astype(BF16), w_ff2[i].astype(BF16),
               g_ple[i][None, :], w_ple_gate[i].astype(BF16), w_ple_proj[i].astype(BF16),
               g_final[None, :])
    return out.reshape(b, s, d)
```

```python
import functools

import jax
import jax.numpy as jnp
from jax import lax
from jax.experimental import pallas as pl
from jax.experimental.pallas import tpu as pltpu

F32 = jnp.float32
BF16 = jnp.bfloat16

D_MODEL = 1024
PLE_DIM = 256
D_CONV = 512
N_HEADS = 4
HEAD_DIM = 128
D_DN = N_HEADS * HEAD_DIM
CHUNK = 64
D_FF = 4 * D_MODEL
EPS = 1e-6

LANES = 128
VMEM_LIMIT = 56 * 1024 * 1024

TM_PROJ = 512
TS_PREP = 512
TC_DELTA = 512
TM_OUT = 512
FF_CHUNK = 1024

CS_BETA, CS_G, CS_EG, CS_EKG, CS_EGT = 0, 8, 16, 24, 32


def _resident(shape):
    nd = len(shape)
    return pl.BlockSpec(shape, lambda *_: (0,) * nd, pipeline_mode=pl.Buffered(1))


def _rms(x, g):
    return x * lax.rsqrt(jnp.mean(x * x, axis=-1, keepdims=True) + EPS) * g


def _dot(a, b):
    return jnp.dot(a, b, preferred_element_type=F32)


def _silu(x):
    return x * jax.nn.sigmoid(x)


def _proj_kernel(x_ref, g_ref, wpa_ref, wqkv_ref, wz_ref, wsm_ref, wg_ref,
                 pa_ref, qkv_ref, z_ref, sm_ref, gates_ref):
    n = _rms(x_ref[...], g_ref[...]).astype(BF16)
    pa_ref[...] = _dot(n, wpa_ref[...])
    qkv_ref[...] = _dot(n, wqkv_ref[...])
    z_ref[...] = _dot(n, wz_ref[...]).astype(z_ref.dtype)
    sm_ref[...] = _dot(n, wsm_ref[...])
    gates_ref[...] = _dot(n, wg_ref[...]).astype(gates_ref.dtype)


def _proj(x2, g_mix, wpa, wqkv, wz, wsm, wg):
    t = x2.shape[0]
    tm = TM_PROJ
    row = lambda w: pl.BlockSpec((tm, w), lambda i: (i, 0))
    return pl.pallas_call(
        _proj_kernel,
        grid=(t // tm,),
        in_specs=[row(D_MODEL), _resident(g_mix.shape), _resident(wpa.shape), _resident(wqkv.shape),
                  _resident(wz.shape), _resident(wsm.shape), _resident(wg.shape)],
        out_specs=[row(3 * D_CONV), row(3 * D_DN), row(D_DN), row(LANES), row(2 * D_MODEL)],
        out_shape=[jax.ShapeDtypeStruct((t, 3 * D_CONV), F32),
                   jax.ShapeDtypeStruct((t, 3 * D_DN), F32),
                   jax.ShapeDtypeStruct((t, D_DN), BF16),
                   jax.ShapeDtypeStruct((t, LANES), F32),
                   jax.ShapeDtypeStruct((t, 2 * D_MODEL), BF16)],
        compiler_params=pltpu.CompilerParams(dimension_semantics=("arbitrary",),
                                             vmem_limit_bytes=VMEM_LIMIT),
        name="proj",
    )(x2, g_mix, wpa, wqkv, wz, wsm, wg)


def _conv3(cur, prev_row, next_row, w):
    ts = cur.shape[0]
    rid = lax.broadcasted_iota(jnp.int32, cur.shape, 0)
    up = jnp.where(rid == 0, prev_row, pltpu.roll(cur, 1, 0))
    dn = jnp.where(rid == ts - 1, next_row, pltpu.roll(cur, ts - 1, 0))
    return up * w[0:1, :] + cur * w[1:2, :] + dn * w[2:3, :]


def _split3(x):
    hi = x.astype(BF16)
    r1 = x - hi.astype(F32)
    mid = r1.astype(BF16)
    lo = (r1 - mid.astype(F32)).astype(BF16)
    return hi, mid, lo


def _prep_kernel(pa_ref, pa_prev_ref, pa_next_ref, qkv_ref, qkv_prev_ref, qkv_next_ref, sm_ref,
                 conv_a_ref, conv_qkv_ref, alog_ref, dtb_ref,
                 ya_ref, q_ref, k_ref, v_ref, cs_ref, gt_ref):
    j = pl.program_id(1)
    nj = pl.num_programs(1)
    ts = TS_PREP
    has_prev = (j > 0).astype(F32)
    has_next = (j < nj - 1).astype(F32)

    pa = pa_ref[0]
    pa_p = pa_prev_ref[0][7:8, :] * has_prev
    pa_n = pa_next_ref[0][0:1, :] * has_next
    u = pa[:, :D_CONV] * pa[:, 2 * D_CONV:]
    u_p = pa_p[:, :D_CONV] * pa_p[:, 2 * D_CONV:]
    u_n = pa_n[:, :D_CONV] * pa_n[:, 2 * D_CONV:]
    ya = pa[:, D_CONV:2 * D_CONV] * _conv3(u, u_p, u_n, conv_a_ref[...])
    ya_ref[0] = ya.astype(ya_ref.dtype)

    qkv = qkv_ref[0]
    qkv_p = qkv_prev_ref[0][7:8, :] * has_prev
    qkv_n = qkv_next_ref[0][0:1, :] * has_next
    c = _silu(_conv3(qkv, qkv_p, qkv_n, conv_qkv_ref[...]))
    for h in range(N_HEADS):
        lo = h * HEAD_DIM
        qh = c[:, lo:lo + HEAD_DIM]
        kh = c[:, D_DN + lo:D_DN + lo + HEAD_DIM]
        qn = qh * lax.rsqrt(jnp.sum(qh * qh, axis=-1, keepdims=True) + EPS) * (HEAD_DIM ** -0.5)
        kn = kh * lax.rsqrt(jnp.sum(kh * kh, axis=-1, keepdims=True) + EPS)
        q_ref[0, :, lo:lo + HEAD_DIM] = qn
        k_ref[0, :, lo:lo + HEAD_DIM] = kn
    v_ref[0] = c[:, 2 * D_DN:]

    sm = sm_ref[0]
    lane = lax.broadcasted_iota(jnp.int32, sm.shape, 1)
    beta = jax.nn.sigmoid(sm)
    xs = sm + dtb_ref[...]
    softplus = jnp.maximum(xs, 0.0) + jnp.log1p(jnp.exp(-jnp.abs(xs)))
    g = jnp.where((lane >= 8) & (lane < 16), -jnp.exp(alog_ref[...]) * softplus, 0.0)

    r = lax.broadcasted_iota(jnp.int32, (ts, ts), 0)
    cc = lax.broadcasted_iota(jnp.int32, (ts, ts), 1)
    same = (r // CHUNK) == (cc // CHUNK)
    m_tot = jnp.where(same, 1.0, 0.0).astype(BF16)
    m_pre = jnp.where(same & (cc <= r), 1.0, 0.0).astype(BF16)
    m_suf = jnp.where(same & (cc >= r), 1.0, 0.0).astype(BF16)
    parts = _split3(g)
    pre = sum(_dot(m_pre, p) for p in parts)
    suf = sum(_dot(m_suf, p) for p in parts)
    tot = sum(_dot(m_tot, p) for p in parts)
    gc = jnp.where(lane < 12, pre, suf)
    eg = jnp.exp(gc)
    ekg = jnp.exp(tot - gc)
    egt = jnp.exp(tot)

    g8 = pltpu.roll(gc, LANES - 8, 1)
    cs = jnp.where(lane < 8, beta, 0.0)
    cs = jnp.where((lane >= 8) & (lane < 16), gc, cs)
    cs = jnp.where((lane >= 16) & (lane < 24), pltpu.roll(eg, 8, 1), cs)
    cs = jnp.where((lane >= 24) & (lane < 32), pltpu.roll(ekg, 16, 1), cs)
    cs = jnp.where((lane >= 32) & (lane < 40), pltpu.roll(egt, 24, 1), cs)
    cs_ref[0] = cs

    g8t = g8.T
    for n in range(ts // CHUNK):
        gt_ref[0, n] = g8t[0:8, n * CHUNK:(n + 1) * CHUNK]


def _prep(pa, qkv, sm, conv_a, conv_qkv, alog, dtb):
    b, s, _ = pa.shape
    ts = TS_PREP
    nj = s // ts
    r8 = ts // 8
    n8 = s // 8

    def cur(w):
        return pl.BlockSpec((1, ts, w), lambda i, j: (i, j, 0))

    def prev(w):
        return pl.BlockSpec((1, 8, w), lambda i, j: (i, jnp.maximum(j * r8 - 1, 0), 0))

    def nxt(w):
        return pl.BlockSpec((1, 8, w), lambda i, j: (i, jnp.minimum((j + 1) * r8, n8 - 1), 0))

    return pl.pallas_call(
        _prep_kernel,
        grid=(b, nj),
        in_specs=[cur(3 * D_CONV), prev(3 * D_CONV), nxt(3 * D_CONV),
                  cur(3 * D_DN), prev(3 * D_DN), nxt(3 * D_DN), cur(LANES),
                  _resident(conv_a.shape), _resident(conv_qkv.shape),
                  _resident(alog.shape), _resident(dtb.shape)],
        out_specs=[cur(D_CONV), cur(D_DN), cur(D_DN), cur(D_DN), cur(LANES),
                   pl.BlockSpec((1, ts // CHUNK, 8, CHUNK), lambda i, j: (i, j, 0, 0))],
        out_shape=[jax.ShapeDtypeStruct((b, s, D_CONV), BF16),
                   jax.ShapeDtypeStruct((b, s, D_DN), F32),
                   jax.ShapeDtypeStruct((b, s, D_DN), F32),
                   jax.ShapeDtypeStruct((b, s, D_DN), F32),
                   jax.ShapeDtypeStruct((b, s, LANES), F32),
                   jax.ShapeDtypeStruct((b, s // CHUNK, 8, CHUNK), F32)],
        compiler_params=pltpu.CompilerParams(dimension_semantics=("arbitrary", "arbitrary"),
                                             vmem_limit_bytes=VMEM_LIMIT),
        name="prep",
    )(pa, pa, pa, qkv, qkv, qkv, sm, conv_a, conv_qkv, alog, dtb)


def _delta_step(chains, s_ref):
    r = lax.broadcasted_iota(jnp.int32, (CHUNK, CHUNK), 0)
    c = lax.broadcasted_iota(jnp.int32, (CHUNK, CHUNK), 1)
    eye = jnp.where(r == c, 1.0, 0.0)

    for ch in chains:
        cs, col = ch["cs"], ch["col"]
        ch["beta"] = cs[:, CS_BETA + col:CS_BETA + col + 1]
        ch["eg"] = cs[:, CS_EG + col:CS_EG + col + 1]
        ch["kb"] = ch["k"] * ch["beta"]
        lhs = jnp.concatenate([ch["kb"], ch["q"]], axis=0).astype(BF16)
        ch["kq"] = lax.dot_general(lhs, ch["k"].astype(BF16), (((1,), (1,)), ((), ())),
                                   preferred_element_type=F32)

    for ch in chains:
        cs, col = ch["cs"], ch["col"]
        incl = (r >= c) if ch["lower"] else (r <= c)
        strict = (r > c) if ch["lower"] else (r < c)
        gcol = cs[:, CS_G + col:CS_G + col + 1]
        decay = jnp.where(incl, jnp.exp(jnp.where(incl, gcol - ch["grow"], 0.0)), 0.0)
        a = jnp.where(strict, ch["kq"][:CHUNK] * decay, 0.0)
        ch["intra"] = (ch["kq"][CHUNK:] * decay).astype(BF16)
        ch["p"] = eye - a
        ch["m"] = a.astype(BF16)

    steps = 1
    while 2 * steps < CHUNK:
        steps *= 2
        for ch in chains:
            ch["m"] = _dot(ch["m"], ch["m"]).astype(BF16)
        for ch in chains:
            ch["p"] = ch["p"] + _dot(ch["p"].astype(BF16), ch["m"])

    for ch in chains:
        rhs = jnp.concatenate([ch["v"] * ch["beta"], ch["kb"] * ch["eg"]], axis=1).astype(BF16)
        ch["uw"] = _dot(ch["p"].astype(BF16), rhs)

    for ch in chains:
        ch["state"] = s_ref[ch["sidx"]]
        lhs = jnp.concatenate([ch["uw"][:, HEAD_DIM:], ch["q"] * ch["eg"]], axis=0).astype(BF16)
        ch["ws"] = _dot(lhs, ch["state"].astype(BF16))

    outs = []
    for ch in chains:
        cs, col = ch["cs"], ch["col"]
        v_new = (ch["uw"][:, :HEAD_DIM] - ch["ws"][:CHUNK]).astype(BF16)
        outs.append(ch["ws"][CHUNK:] + _dot(ch["intra"], v_new))
        ekg = cs[:, CS_EKG + col:CS_EKG + col + 1]
        egt = cs[0:1, CS_EGT + col:CS_EGT + col + 1]
        kg = (ch["k"] * ekg).astype(BF16)
        upd = lax.dot_general(kg, v_new, (((0,), (0,)), ((), ())), preferred_element_type=F32)
        s_ref[ch["sidx"]] = ch["state"] * egt + upd
    return outs


def _delta_kernel(qf_ref, kf_ref, vf_ref, csf_ref, gtf_ref, qb_ref, kb_ref, vb_ref, csb_ref, gtb_ref,
                  of_ref, ob_ref, s_ref):
    @pl.when(pl.program_id(1) == 0)
    def _():
        s_ref[...] = jnp.zeros_like(s_ref)

    nc = TC_DELTA // CHUNK

    def body(i, carry):
        rf = pl.multiple_of(i * CHUNK, CHUNK)
        ib = nc - 1 - i
        rb = pl.multiple_of(ib * CHUNK, CHUNK)
        csf = csf_ref[0, pl.ds(rf, CHUNK), :]
        csb = csb_ref[0, pl.ds(rb, CHUNK), :]
        gtf = gtf_ref[0, i]
        gtb = gtb_ref[0, ib]
        chains = []
        for h in range(N_HEADS):
            sl = slice(h * HEAD_DIM, (h + 1) * HEAD_DIM)
            chains.append(dict(q=qf_ref[0, pl.ds(rf, CHUNK), sl], k=kf_ref[0, pl.ds(rf, CHUNK), sl],
                               v=vf_ref[0, pl.ds(rf, CHUNK), sl], cs=csf, grow=gtf[h:h + 1, :],
                               col=h, sidx=h, lower=True))
            chains.append(dict(q=qb_ref[0, pl.ds(rb, CHUNK), sl], k=kb_ref[0, pl.ds(rb, CHUNK), sl],
                               v=vb_ref[0, pl.ds(rb, CHUNK), sl], cs=csb, grow=gtb[4 + h:5 + h, :],
                               col=4 + h, sidx=4 + h, lower=False))
        outs = _delta_step(chains, s_ref)
        for h in range(N_HEADS):
            sl = slice(h * HEAD_DIM, (h + 1) * HEAD_DIM)
            of_ref[0, pl.ds(rf, CHUNK), sl] = outs[2 * h]
            ob_ref[0, pl.ds(rb, CHUNK), sl] = outs[2 * h + 1]
        return carry

    lax.fori_loop(0, nc, body, 0)


def _delta(q, k, v, cs, gt):
    b, s, _ = q.shape
    tc = TC_DELTA
    nb = s // tc
    ncb = tc // CHUNK
    fwd = lambda w: pl.BlockSpec((1, tc, w), lambda i, j: (i, j, 0))
    bwd = lambda w: pl.BlockSpec((1, tc, w), lambda i, j: (i, nb - 1 - j, 0))
    gt_f = pl.BlockSpec((1, ncb, 8, CHUNK), lambda i, j: (i, j, 0, 0))
    gt_b = pl.BlockSpec((1, ncb, 8, CHUNK), lambda i, j: (i, nb - 1 - j, 0, 0))
    return pl.pallas_call(
        _delta_kernel,
        grid=(b, nb),
        in_specs=[fwd(D_DN), fwd(D_DN), fwd(D_DN), fwd(LANES), gt_f,
                  bwd(D_DN), bwd(D_DN), bwd(D_DN), bwd(LANES), gt_b],
        out_specs=[fwd(D_DN), bwd(D_DN)],
        out_shape=[jax.ShapeDtypeStruct((b, s, D_DN), F32), jax.ShapeDtypeStruct((b, s, D_DN), F32)],
        scratch_shapes=[pltpu.VMEM((2 * N_HEADS, HEAD_DIM, HEAD_DIM), F32)],
        compiler_params=pltpu.CompilerParams(dimension_semantics=("arbitrary", "arbitrary"),
                                             vmem_limit_bytes=VMEM_LIMIT),
        name="delta",
    )(q, k, v, cs, gt, q, k, v, cs, gt)


def _out_kernel(x_ref, p_ref, of_ref, ob_ref, z_ref, ya_ref, gates_ref,
                gdn_ref, wa_ref, wdn_ref, wo_ref, gffn_ref, w1_ref, w2_ref,
                gple_ref, wpg_ref, wpp_ref, gfin_ref, out_ref):
    o = of_ref[...] + ob_ref[...]
    z = z_ref[...].astype(F32)
    gdn = gdn_ref[...]
    parts = []
    for h in range(N_HEADS):
        sl = slice(h * HEAD_DIM, (h + 1) * HEAD_DIM)
        parts.append(_rms(o[:, sl], gdn) * _silu(z[:, sl]))
    y_dn = jnp.concatenate(parts, axis=1).astype(BF16)
    gates = gates_ref[...].astype(F32)
    merged = (jax.nn.sigmoid(gates[:, :D_MODEL]) * _dot(ya_ref[...], wa_ref[...])
              + jax.nn.sigmoid(gates[:, D_MODEL:]) * _dot(y_dn, wdn_ref[...]))
    h1 = x_ref[...] + _dot(merged.astype(BF16), wo_ref[...])

    n2 = _rms(h1, gffn_ref[...]).astype(BF16)
    h2 = h1
    for c in range(D_FF // FF_CHUNK):
        sl = slice(c * FF_CHUNK, (c + 1) * FF_CHUNK)
        a = jnp.maximum(_dot(n2, w1_ref[:, sl]), 0.0)
        h2 = h2 + _dot((a * a).astype(BF16), w2_ref[sl, :])

    n3 = _rms(h2, gple_ref[...]).astype(BF16)
    gate_p = jax.nn.sigmoid(_dot(n3, wpg_ref[...]))
    h3 = h2 + gate_p * _dot(p_ref[...].astype(BF16), wpp_ref[...])
    out_ref[...] = _rms(h3, gfin_ref[...])


def _out(x2, p2, o_f, o_b, z, ya, gates, gdn, wa, wdn, wo, gffn, w1, w2, gple, wpg, wpp, gfin):
    t = x2.shape[0]
    tm = TM_OUT
    row = lambda w: pl.BlockSpec((tm, w), lambda i: (i, 0))
    weights = (gdn, wa, wdn, wo, gffn, w1, w2, gple, wpg, wpp, gfin)
    return pl.pallas_call(
        _out_kernel,
        grid=(t // tm,),
        in_specs=[row(D_MODEL), row(PLE_DIM), row(D_DN), row(D_DN), row(D_DN), row(D_CONV),
                  row(2 * D_MODEL)] + [_resident(w.shape) for w in weights],
        out_specs=row(D_MODEL),
        out_shape=jax.ShapeDtypeStruct((t, D_MODEL), F32),
        compiler_params=pltpu.CompilerParams(dimension_semantics=("arbitrary",),
                                             vmem_limit_bytes=VMEM_LIMIT),
        name="out",
    )(x2, p2, o_f, o_b, z, ya, gates, *weights)


def kernel(x, p, g_mix, w_in, conv_a, conv_qkv, a_log_f, a_log_b, dt_bias_f, dt_bias_b, g_dn_norm,
           w_a_out, w_dn_out, w_o, g_ffn, w_ff1, w_ff2, g_ple, w_ple_gate, w_ple_proj, g_final):
    b, s, d = x.shape
    depth = p.shape[0]
    assert depth == 1 and d == D_MODEL and s % TC_DELTA == 0 and s % TS_PREP == 0
    t = b * s
    i = 0

    w = w_in[i]
    c0 = 3 * D_CONV
    c1 = c0 + 3 * D_DN
    c2 = c1 + D_DN
    c3 = c2 + 4 * N_HEADS
    wpa = w[:, :c0].astype(BF16)
    wqkv = w[:, c0:c1].astype(BF16)
    wz = w[:, c1:c2].astype(BF16)
    wsm = jnp.pad(w[:, c2:c3], ((0, 0), (0, LANES - 4 * N_HEADS))).astype(BF16)
    wg = w[:, c3:].astype(BF16)

    x2 = x.reshape(t, d)
    pa, qkv, z, sm, gates = _proj(x2, g_mix[i][None, :], wpa, wqkv, wz, wsm, wg)

    zeros8 = jnp.zeros((2 * N_HEADS,), F32)
    tail = jnp.zeros((LANES - 4 * N_HEADS,), F32)
    alog = jnp.concatenate([zeros8, a_log_f[i], a_log_b[i], tail])[None, :]
    dtb = jnp.concatenate([zeros8, dt_bias_f[i], dt_bias_b[i], tail])[None, :]
    ya, q, k, v, cs, gt = _prep(pa.reshape(b, s, -1), qkv.reshape(b, s, -1), sm.reshape(b, s, -1),
                                conv_a[i], conv_qkv[i], alog, dtb)

    o_f, o_b = _delta(q, k, v, cs, gt)

    out = _out(x2, p[i].reshape(t, PLE_DIM), o_f.reshape(t, D_DN), o_b.reshape(t, D_DN), z,
               ya.reshape(t, D_CONV), gates,
               g_dn_norm[i][None, :], w_a_out[i].astype(BF16), w_dn_out[i].astype(BF16),
               w_o[i].astype(BF16), g_ffn[i][None, :], w_ff1[i].astype(BF16), w_ff2[i].astype(BF16),
               g_ple[i][None, :], w_ple_gate[i].astype(BF16), w_ple_proj[i].astype(BF16),
               g_final[None, :])
    return out.reshape(b, s, d)
```

```python
import functools

import jax
import jax.numpy as jnp
from jax import lax
from jax.experimental import pallas as pl
from jax.experimental.pallas import tpu as pltpu

F32 = jnp.float32
BF16 = jnp.bfloat16

D_MODEL = 1024
PLE_DIM = 256
D_CONV = 512
N_HEADS = 4
HEAD_DIM = 128
D_DN = N_HEADS * HEAD_DIM
CHUNK = 64
D_FF = 4 * D_MODEL
EPS = 1e-6

LANES = 128
VMEM_LIMIT = 56 * 1024 * 1024

TM_PROJ = 512
TS_PREP = 512
TC_DELTA = 256
TM_OUT = 512
FF_CHUNK = 1024

CS_BETA, CS_G, CS_EG, CS_EKG, CS_EGT = 0, 8, 16, 24, 32


def _resident(shape):
    nd = len(shape)
    return pl.BlockSpec(shape, lambda *_: (0,) * nd, pipeline_mode=pl.Buffered(1))


def _rms(x, g):
    return x * lax.rsqrt(jnp.mean(x * x, axis=-1, keepdims=True) + EPS) * g


def _dot(a, b):
    return jnp.dot(a, b, preferred_element_type=F32)


def _silu(x):
    return x * jax.nn.sigmoid(x)


def _proj_kernel(x_ref, g_ref, wpa_ref, wqkv_ref, wz_ref, wsm_ref, wg_ref,
                 pa_ref, qkv_ref, z_ref, sm_ref, gates_ref):
    n = _rms(x_ref[...], g_ref[...]).astype(BF16)
    pa_ref[...] = _dot(n, wpa_ref[...])
    qkv_ref[...] = _dot(n, wqkv_ref[...])
    z_ref[...] = _dot(n, wz_ref[...]).astype(z_ref.dtype)
    sm_ref[...] = _dot(n, wsm_ref[...])
    gates_ref[...] = _dot(n, wg_ref[...]).astype(gates_ref.dtype)


def _proj(x2, g_mix, wpa, wqkv, wz, wsm, wg):
    t = x2.shape[0]
    tm = TM_PROJ
    row = lambda w: pl.BlockSpec((tm, w), lambda i: (i, 0))
    return pl.pallas_call(
        _proj_kernel,
        grid=(t // tm,),
        in_specs=[row(D_MODEL), _resident(g_mix.shape), _resident(wpa.shape), _resident(wqkv.shape),
                  _resident(wz.shape), _resident(wsm.shape), _resident(wg.shape)],
        out_specs=[row(3 * D_CONV), row(3 * D_DN), row(D_DN), row(LANES), row(2 * D_MODEL)],
        out_shape=[jax.ShapeDtypeStruct((t, 3 * D_CONV), F32),
                   jax.ShapeDtypeStruct((t, 3 * D_DN), F32),
                   jax.ShapeDtypeStruct((t, D_DN), BF16),
                   jax.ShapeDtypeStruct((t, LANES), F32),
                   jax.ShapeDtypeStruct((t, 2 * D_MODEL), BF16)],
        compiler_params=pltpu.CompilerParams(dimension_semantics=("arbitrary",),
                                             vmem_limit_bytes=VMEM_LIMIT),
        name="proj",
    )(x2, g_mix, wpa, wqkv, wz, wsm, wg)


def _conv3(cur, prev_row, next_row, w):
    ts = cur.shape[0]
    rid = lax.broadcasted_iota(jnp.int32, cur.shape, 0)
    up = jnp.where(rid == 0, prev_row, pltpu.roll(cur, 1, 0))
    dn = jnp.where(rid == ts - 1, next_row, pltpu.roll(cur, ts - 1, 0))
    return up * w[0:1, :] + cur * w[1:2, :] + dn * w[2:3, :]


def _split3(x):
    hi = x.astype(BF16)
    r1 = x - hi.astype(F32)
    mid = r1.astype(BF16)
    lo = (r1 - mid.astype(F32)).astype(BF16)
    return hi, mid, lo


def _prep_kernel(pa_ref, pa_prev_ref, pa_next_ref, qkv_ref, qkv_prev_ref, qkv_next_ref, sm_ref,
                 conv_a_ref, conv_qkv_ref, alog_ref, dtb_ref,
                 ya_ref, q_ref, k_ref, v_ref, cs_ref, gt_ref):
    j = pl.program_id(1)
    nj = pl.num_programs(1)
    ts = TS_PREP
    has_prev = (j > 0).astype(F32)
    has_next = (j < nj - 1).astype(F32)

    pa = pa_ref[0]
    pa_p = pa_prev_ref[0][7:8, :] * has_prev
    pa_n = pa_next_ref[0][0:1, :] * has_next
    u = pa[:, :D_CONV] * pa[:, 2 * D_CONV:]
    u_p = pa_p[:, :D_CONV] * pa_p[:, 2 * D_CONV:]
    u_n = pa_n[:, :D_CONV] * pa_n[:, 2 * D_CONV:]
    ya = pa[:, D_CONV:2 * D_CONV] * _conv3(u, u_p, u_n, conv_a_ref[...])
    ya_ref[0] = ya.astype(ya_ref.dtype)

    qkv = qkv_ref[0]
    qkv_p = qkv_prev_ref[0][7:8, :] * has_prev
    qkv_n = qkv_next_ref[0][0:1, :] * has_next
    c = _silu(_conv3(qkv, qkv_p, qkv_n, conv_qkv_ref[...]))
    for h in range(N_HEADS):
        lo = h * HEAD_DIM
        qh = c[:, lo:lo + HEAD_DIM]
        kh = c[:, D_DN + lo:D_DN + lo + HEAD_DIM]
        qn = qh * lax.rsqrt(jnp.sum(qh * qh, axis=-1, keepdims=True) + EPS) * (HEAD_DIM ** -0.5)
        kn = kh * lax.rsqrt(jnp.sum(kh * kh, axis=-1, keepdims=True) + EPS)
        q_ref[0, :, lo:lo + HEAD_DIM] = qn
        k_ref[0, :, lo:lo + HEAD_DIM] = kn
    v_ref[0] = c[:, 2 * D_DN:]

    sm = sm_ref[0]
    lane = lax.broadcasted_iota(jnp.int32, sm.shape, 1)
    beta = jax.nn.sigmoid(sm)
    xs = sm + dtb_ref[...]
    softplus = jnp.maximum(xs, 0.0) + jnp.log1p(jnp.exp(-jnp.abs(xs)))
    g = jnp.where((lane >= 8) & (lane < 16), -jnp.exp(alog_ref[...]) * softplus, 0.0)

    r = lax.broadcasted_iota(jnp.int32, (ts, ts), 0)
    cc = lax.broadcasted_iota(jnp.int32, (ts, ts), 1)
    same = (r // CHUNK) == (cc // CHUNK)
    m_tot = jnp.where(same, 1.0, 0.0).astype(BF16)
    m_pre = jnp.where(same & (cc <= r), 1.0, 0.0).astype(BF16)
    m_suf = jnp.where(same & (cc >= r), 1.0, 0.0).astype(BF16)
    parts = _split3(g)
    pre = sum(_dot(m_pre, p) for p in parts)
    suf = sum(_dot(m_suf, p) for p in parts)
    tot = sum(_dot(m_tot, p) for p in parts)
    gc = jnp.where(lane < 12, pre, suf)
    eg = jnp.exp(gc)
    ekg = jnp.exp(tot - gc)
    egt = jnp.exp(tot)

    g8 = pltpu.roll(gc, LANES - 8, 1)
    cs = jnp.where(lane < 8, beta, 0.0)
    cs = jnp.where((lane >= 8) & (lane < 16), gc, cs)
    cs = jnp.where((lane >= 16) & (lane < 24), pltpu.roll(eg, 8, 1), cs)
    cs = jnp.where((lane >= 24) & (lane < 32), pltpu.roll(ekg, 16, 1), cs)
    cs = jnp.where((lane >= 32) & (lane < 40), pltpu.roll(egt, 24, 1), cs)
    cs_ref[0] = cs

    g8t = g8.T
    for n in range(ts // CHUNK):
        gt_ref[0, n] = g8t[0:8, n * CHUNK:(n + 1) * CHUNK]


def _prep(pa, qkv, sm, conv_a, conv_qkv, alog, dtb):
    b, s, _ = pa.shape
    ts = TS_PREP
    nj = s // ts
    r8 = ts // 8
    n8 = s // 8

    def cur(w):
        return pl.BlockSpec((1, ts, w), lambda i, j: (i, j, 0))

    def prev(w):
        return pl.BlockSpec((1, 8, w), lambda i, j: (i, jnp.maximum(j * r8 - 1, 0), 0))

    def nxt(w):
        return pl.BlockSpec((1, 8, w), lambda i, j: (i, jnp.minimum((j + 1) * r8, n8 - 1), 0))

    return pl.pallas_call(
        _prep_kernel,
        grid=(b, nj),
        in_specs=[cur(3 * D_CONV), prev(3 * D_CONV), nxt(3 * D_CONV),
                  cur(3 * D_DN), prev(3 * D_DN), nxt(3 * D_DN), cur(LANES),
                  _resident(conv_a.shape), _resident(conv_qkv.shape),
                  _resident(alog.shape), _resident(dtb.shape)],
        out_specs=[cur(D_CONV), cur(D_DN), cur(D_DN), cur(D_DN), cur(LANES),
                   pl.BlockSpec((1, ts // CHUNK, 8, CHUNK), lambda i, j: (i, j, 0, 0))],
        out_shape=[jax.ShapeDtypeStruct((b, s, D_CONV), BF16),
                   jax.ShapeDtypeStruct((b, s, D_DN), F32),
                   jax.ShapeDtypeStruct((b, s, D_DN), F32),
                   jax.ShapeDtypeStruct((b, s, D_DN), F32),
                   jax.ShapeDtypeStruct((b, s, LANES), F32),
                   jax.ShapeDtypeStruct((b, s // CHUNK, 8, CHUNK), F32)],
        compiler_params=pltpu.CompilerParams(dimension_semantics=("arbitrary", "arbitrary"),
                                             vmem_limit_bytes=VMEM_LIMIT),
        name="prep",
    )(pa, pa, pa, qkv, qkv, qkv, sm, conv_a, conv_qkv, alog, dtb)


def _delta_step(chains, s_ref):
    r = lax.broadcasted_iota(jnp.int32, (CHUNK, CHUNK), 0)
    c = lax.broadcasted_iota(jnp.int32, (CHUNK, CHUNK), 1)
    eye = jnp.where(r == c, 1.0, 0.0)

    for ch in chains:
        cs, col = ch["cs"], ch["col"]
        ch["beta"] = cs[:, CS_BETA + col:CS_BETA + col + 1]
        ch["eg"] = cs[:, CS_EG + col:CS_EG + col + 1]
        ch["kb"] = ch["k"] * ch["beta"]
        lhs = jnp.concatenate([ch["kb"], ch["q"]], axis=0).astype(BF16)
        ch["kq"] = lax.dot_general(lhs, ch["k"].astype(BF16), (((1,), (1,)), ((), ())),
                                   preferred_element_type=F32)

    for ch in chains:
        cs, col = ch["cs"], ch["col"]
        incl = (r >= c) if ch["lower"] else (r <= c)
        strict = (r > c) if ch["lower"] else (r < c)
        gcol = cs[:, CS_G + col:CS_G + col + 1]
        decay = jnp.where(incl, jnp.exp(jnp.where(incl, gcol - ch["grow"], 0.0)), 0.0)
        a = jnp.where(strict, ch["kq"][:CHUNK] * decay, 0.0)
        ch["intra"] = (ch["kq"][CHUNK:] * decay).astype(BF16)
        ch["p"] = eye - a
        ch["m"] = a.astype(BF16)

    steps = 1
    while 2 * steps < CHUNK:
        steps *= 2
        for ch in chains:
            ch["m"] = _dot(ch["m"], ch["m"]).astype(BF16)
        for ch in chains:
            ch["p"] = ch["p"] + _dot(ch["p"].astype(BF16), ch["m"])

    for ch in chains:
        rhs = jnp.concatenate([ch["v"] * ch["beta"], ch["kb"] * ch["eg"]], axis=1).astype(BF16)
        ch["uw"] = _dot(ch["p"].astype(BF16), rhs)

    for ch in chains:
        ch["state"] = s_ref[ch["sidx"]]
        lhs = jnp.concatenate([ch["uw"][:, HEAD_DIM:], ch["q"] * ch["eg"]], axis=0).astype(BF16)
        ch["ws"] = _dot(lhs, ch["state"].astype(BF16))

    outs = []
    for ch in chains:
        cs, col = ch["cs"], ch["col"]
        v_new = (ch["uw"][:, :HEAD_DIM] - ch["ws"][:CHUNK]).astype(BF16)
        outs.append(ch["ws"][CHUNK:] + _dot(ch["intra"], v_new))
        ekg = cs[:, CS_EKG + col:CS_EKG + col + 1]
        egt = cs[0:1, CS_EGT + col:CS_EGT + col + 1]
        kg = (ch["k"] * ekg).astype(BF16)
        upd = lax.dot_general(kg, v_new, (((0,), (0,)), ((), ())), preferred_element_type=F32)
        s_ref[ch["sidx"]] = ch["state"] * egt + upd
    return outs


def _delta_kernel(qf_ref, kf_ref, vf_ref, csf_ref, gtf_ref, qb_ref, kb_ref, vb_ref, csb_ref, gtb_ref,
                  of_ref, ob_ref, s_ref):
    @pl.when(pl.program_id(0) == 0)
    def _():
        s_ref[...] = jnp.zeros_like(s_ref)

    nbatch = qf_ref.shape[0]
    nc = TC_DELTA // CHUNK

    def body(i, carry):
        rf = pl.multiple_of(i * CHUNK, CHUNK)
        ib = nc - 1 - i
        rb = pl.multiple_of(ib * CHUNK, CHUNK)
        chains = []
        for bi in range(nbatch):
            csf = csf_ref[bi, pl.ds(rf, CHUNK), :]
            csb = csb_ref[bi, pl.ds(rb, CHUNK), :]
            gtf = gtf_ref[bi, i]
            gtb = gtb_ref[bi, ib]
            for h in range(N_HEADS):
                sl = slice(h * HEAD_DIM, (h + 1) * HEAD_DIM)
                chains.append(dict(q=qf_ref[bi, pl.ds(rf, CHUNK), sl], k=kf_ref[bi, pl.ds(rf, CHUNK), sl],
                                   v=vf_ref[bi, pl.ds(rf, CHUNK), sl], cs=csf, grow=gtf[h:h + 1, :],
                                   col=h, sidx=bi * 2 * N_HEADS + h, lower=True))
                chains.append(dict(q=qb_ref[bi, pl.ds(rb, CHUNK), sl], k=kb_ref[bi, pl.ds(rb, CHUNK), sl],
                                   v=vb_ref[bi, pl.ds(rb, CHUNK), sl], cs=csb, grow=gtb[4 + h:5 + h, :],
                                   col=4 + h, sidx=bi * 2 * N_HEADS + 4 + h, lower=False))
        outs = _delta_step(chains, s_ref)
        n = 0
        for bi in range(nbatch):
            for h in range(N_HEADS):
                sl = slice(h * HEAD_DIM, (h + 1) * HEAD_DIM)
                of_ref[bi, pl.ds(rf, CHUNK), sl] = outs[n]
                ob_ref[bi, pl.ds(rb, CHUNK), sl] = outs[n + 1]
                n += 2
        return carry

    lax.fori_loop(0, nc, body, 0)


def _delta(q, k, v, cs, gt):
    b, s, _ = q.shape
    tc = TC_DELTA
    nb = s // tc
    ncb = tc // CHUNK
    fwd = lambda w: pl.BlockSpec((b, tc, w), lambda j: (0, j, 0))
    bwd = lambda w: pl.BlockSpec((b, tc, w), lambda j: (0, nb - 1 - j, 0))
    gt_f = pl.BlockSpec((b, ncb, 8, CHUNK), lambda j: (0, j, 0, 0))
    gt_b = pl.BlockSpec((b, ncb, 8, CHUNK), lambda j: (0, nb - 1 - j, 0, 0))
    return pl.pallas_call(
        _delta_kernel,
        grid=(nb,),
        in_specs=[fwd(D_DN), fwd(D_DN), fwd(D_DN), fwd(LANES), gt_f,
                  bwd(D_DN), bwd(D_DN), bwd(D_DN), bwd(LANES), gt_b],
        out_specs=[fwd(D_DN), bwd(D_DN)],
        out_shape=[jax.ShapeDtypeStruct((b, s, D_DN), F32), jax.ShapeDtypeStruct((b, s, D_DN), F32)],
        scratch_shapes=[pltpu.VMEM((b * 2 * N_HEADS, HEAD_DIM, HEAD_DIM), F32)],
        compiler_params=pltpu.CompilerParams(dimension_semantics=("arbitrary",),
                                             vmem_limit_bytes=VMEM_LIMIT),
        name="delta",
    )(q, k, v, cs, gt, q, k, v, cs, gt)


def _out_kernel(x_ref, p_ref, of_ref, ob_ref, z_ref, ya_ref, gates_ref,
                gdn_ref, wa_ref, wdn_ref, wo_ref, gffn_ref, w1_ref, w2_ref,
                gple_ref, wpg_ref, wpp_ref, gfin_ref, out_ref):
    o = of_ref[...] + ob_ref[...]
    z = z_ref[...].astype(F32)
    gdn = gdn_ref[...]
    parts = []
    for h in range(N_HEADS):
        sl = slice(h * HEAD_DIM, (h + 1) * HEAD_DIM)
        parts.append(_rms(o[:, sl], gdn) * _silu(z[:, sl]))
    y_dn = jnp.concatenate(parts, axis=1).astype(BF16)
    gates = gates_ref[...].astype(F32)
    merged = (jax.nn.sigmoid(gates[:, :D_MODEL]) * _dot(ya_ref[...], wa_ref[...])
              + jax.nn.sigmoid(gates[:, D_MODEL:]) * _dot(y_dn, wdn_ref[...]))
    h1 = x_ref[...] + _dot(merged.astype(BF16), wo_ref[...])

    n2 = _rms(h1, gffn_ref[...]).astype(BF16)
    h2 = h1
    for c in range(D_FF // FF_CHUNK):
        sl = slice(c * FF_CHUNK, (c + 1) * FF_CHUNK)
        a = jnp.maximum(_dot(n2, w1_ref[:, sl]), 0.0)
        h2 = h2 + _dot((a * a).astype(BF16), w2_ref[sl, :])

    n3 = _rms(h2, gple_ref[...]).astype(BF16)
    gate_p = jax.nn.sigmoid(_dot(n3, wpg_ref[...]))
    h3 = h2 + gate_p * _dot(p_ref[...].astype(BF16), wpp_ref[...])
    out_ref[...] = _rms(h3, gfin_ref[...])


def _out(x2, p2, o_f, o_b, z, ya, gates, gdn, wa, wdn, wo, gffn, w1, w2, gple, wpg, wpp, gfin):
    t = x2.shape[0]
    tm = TM_OUT
    row = lambda w: pl.BlockSpec((tm, w), lambda i: (i, 0))
    weights = (gdn, wa, wdn, wo, gffn, w1, w2, gple, wpg, wpp, gfin)
    return pl.pallas_call(
        _out_kernel,
        grid=(t // tm,),
        in_specs=[row(D_MODEL), row(PLE_DIM), row(D_DN), row(D_DN), row(D_DN), row(D_CONV),
                  row(2 * D_MODEL)] + [_resident(w.shape) for w in weights],
        out_specs=row(D_MODEL),
        out_shape=jax.ShapeDtypeStruct((t, D_MODEL), F32),
        compiler_params=pltpu.CompilerParams(dimension_semantics=("arbitrary",),
                                             vmem_limit_bytes=VMEM_LIMIT),
        name="out",
    )(x2, p2, o_f, o_b, z, ya, gates, *weights)


def kernel(x, p, g_mix, w_in, conv_a, conv_qkv, a_log_f, a_log_b, dt_bias_f, dt_bias_b, g_dn_norm,
           w_a_out, w_dn_out, w_o, g_ffn, w_ff1, w_ff2, g_ple, w_ple_gate, w_ple_proj, g_final):
    b, s, d = x.shape
    depth = p.shape[0]
    assert depth == 1 and d == D_MODEL and s % TC_DELTA == 0 and s % TS_PREP == 0
    t = b * s
    i = 0

    w = w_in[i]
    c0 = 3 * D_CONV
    c1 = c0 + 3 * D_DN
    c2 = c1 + D_DN
    c3 = c2 + 4 * N_HEADS
    wpa = w[:, :c0].astype(BF16)
    wqkv = w[:, c0:c1].astype(BF16)
    wz = w[:, c1:c2].astype(BF16)
    wsm = jnp.pad(w[:, c2:c3], ((0, 0), (0, LANES - 4 * N_HEADS))).astype(BF16)
    wg = w[:, c3:].astype(BF16)

    x2 = x.reshape(t, d)
    pa, qkv, z, sm, gates = _proj(x2, g_mix[i][None, :], wpa, wqkv, wz, wsm, wg)

    zeros8 = jnp.zeros((2 * N_HEADS,), F32)
    tail = jnp.zeros((LANES - 4 * N_HEADS,), F32)
    alog = jnp.concatenate([zeros8, a_log_f[i], a_log_b[i], tail])[None, :]
    dtb = jnp.concatenate([zeros8, dt_bias_f[i], dt_bias_b[i], tail])[None, :]
    ya, q, k, v, cs, gt = _prep(pa.reshape(b, s, -1), qkv.reshape(b, s, -1), sm.reshape(b, s, -1),
                                conv_a[i], conv_qkv[i], alog, dtb)

    o_f, o_b = _delta(q, k, v, cs, gt)

    out = _out(x2, p[i].reshape(t, PLE_DIM), o_f.reshape(t, D_DN), o_b.reshape(t, D_DN), z,
               ya.reshape(t, D_CONV), gates,
               g_dn_norm[i][None, :], w_a_out[i].astype(BF16), w_dn_out[i].astype(BF16),
               w_o[i].astype(BF16), g_ffn[i][None, :], w_ff1[i].astype(BF16), w_ff2[i].astype(BF16),
               g_ple[i][None, :], w_ple_gate[i].astype(BF16), w_ple_proj[i].astype(BF16),
               g_final[None, :])
    return out.reshape(b, s, d)
```

```python
import functools

import jax
import jax.numpy as jnp
from jax import lax
from jax.experimental import pallas as pl
from jax.experimental.pallas import tpu as pltpu

F32 = jnp.float32
BF16 = jnp.bfloat16

D_MODEL = 1024
PLE_DIM = 256
D_CONV = 512
N_HEADS = 4
HEAD_DIM = 128
D_DN = N_HEADS * HEAD_DIM
CHUNK = 64
D_FF = 4 * D_MODEL
EPS = 1e-6

LANES = 128
VMEM_LIMIT = 56 * 1024 * 1024

TM_PROJ = 512
TS_PREP = 512
TC_DELTA = 256
DELTA_CHUNKS_PER_ITER = 2
TM_OUT = 512
FF_CHUNK = 1024

CS_BETA, CS_G, CS_EG, CS_EKG, CS_EGT = 0, 8, 16, 24, 32


def _resident(shape):
    nd = len(shape)
    return pl.BlockSpec(shape, lambda *_: (0,) * nd, pipeline_mode=pl.Buffered(1))


def _rms(x, g):
    return x * lax.rsqrt(jnp.mean(x * x, axis=-1, keepdims=True) + EPS) * g


def _dot(a, b):
    return jnp.dot(a, b, preferred_element_type=F32)


def _silu(x):
    return x * jax.nn.sigmoid(x)


def _proj_kernel(x_ref, g_ref, wpa_ref, wqkv_ref, wz_ref, wsm_ref, wg_ref,
                 pa_ref, qkv_ref, z_ref, sm_ref, gates_ref):
    n = _rms(x_ref[...], g_ref[...]).astype(BF16)
    pa_ref[...] = _dot(n, wpa_ref[...])
    qkv_ref[...] = _dot(n, wqkv_ref[...])
    z_ref[...] = _dot(n, wz_ref[...]).astype(z_ref.dtype)
    sm_ref[...] = _dot(n, wsm_ref[...])
    gates_ref[...] = _dot(n, wg_ref[...]).astype(gates_ref.dtype)


def _proj(x2, g_mix, wpa, wqkv, wz, wsm, wg):
    t = x2.shape[0]
    tm = TM_PROJ
    row = lambda w: pl.BlockSpec((tm, w), lambda i: (i, 0))
    return pl.pallas_call(
        _proj_kernel,
        grid=(t // tm,),
        in_specs=[row(D_MODEL), _resident(g_mix.shape), _resident(wpa.shape), _resident(wqkv.shape),
                  _resident(wz.shape), _resident(wsm.shape), _resident(wg.shape)],
        out_specs=[row(3 * D_CONV), row(3 * D_DN), row(D_DN), row(LANES), row(2 * D_MODEL)],
        out_shape=[jax.ShapeDtypeStruct((t, 3 * D_CONV), F32),
                   jax.ShapeDtypeStruct((t, 3 * D_DN), F32),
                   jax.ShapeDtypeStruct((t, D_DN), BF16),
                   jax.ShapeDtypeStruct((t, LANES), F32),
                   jax.ShapeDtypeStruct((t, 2 * D_MODEL), BF16)],
        compiler_params=pltpu.CompilerParams(dimension_semantics=("arbitrary",),
                                             vmem_limit_bytes=VMEM_LIMIT),
        name="proj",
    )(x2, g_mix, wpa, wqkv, wz, wsm, wg)


def _conv3(cur, prev_row, next_row, w):
    ts = cur.shape[0]
    rid = lax.broadcasted_iota(jnp.int32, cur.shape, 0)
    up = jnp.where(rid == 0, prev_row, pltpu.roll(cur, 1, 0))
    dn = jnp.where(rid == ts - 1, next_row, pltpu.roll(cur, ts - 1, 0))
    return up * w[0:1, :] + cur * w[1:2, :] + dn * w[2:3, :]


def _split3(x):
    hi = x.astype(BF16)
    r1 = x - hi.astype(F32)
    mid = r1.astype(BF16)
    lo = (r1 - mid.astype(F32)).astype(BF16)
    return hi, mid, lo


def _prep_kernel(pa_ref, pa_prev_ref, pa_next_ref, qkv_ref, qkv_prev_ref, qkv_next_ref, sm_ref,
                 conv_a_ref, conv_qkv_ref, alog_ref, dtb_ref,
                 ya_ref, q_ref, k_ref, v_ref, cs_ref, gt_ref):
    j = pl.program_id(1)
    nj = pl.num_programs(1)
    ts = TS_PREP
    has_prev = (j > 0).astype(F32)
    has_next = (j < nj - 1).astype(F32)

    pa = pa_ref[0]
    pa_p = pa_prev_ref[0][7:8, :] * has_prev
    pa_n = pa_next_ref[0][0:1, :] * has_next
    u = pa[:, :D_CONV] * pa[:, 2 * D_CONV:]
    u_p = pa_p[:, :D_CONV] * pa_p[:, 2 * D_CONV:]
    u_n = pa_n[:, :D_CONV] * pa_n[:, 2 * D_CONV:]
    ya = pa[:, D_CONV:2 * D_CONV] * _conv3(u, u_p, u_n, conv_a_ref[...])
    ya_ref[0] = ya.astype(ya_ref.dtype)

    qkv = qkv_ref[0]
    qkv_p = qkv_prev_ref[0][7:8, :] * has_prev
    qkv_n = qkv_next_ref[0][0:1, :] * has_next
    c = _silu(_conv3(qkv, qkv_p, qkv_n, conv_qkv_ref[...]))
    for h in range(N_HEADS):
        lo = h * HEAD_DIM
        qh = c[:, lo:lo + HEAD_DIM]
        kh = c[:, D_DN + lo:D_DN + lo + HEAD_DIM]
        qn = qh * lax.rsqrt(jnp.sum(qh * qh, axis=-1, keepdims=True) + EPS) * (HEAD_DIM ** -0.5)
        kn = kh * lax.rsqrt(jnp.sum(kh * kh, axis=-1, keepdims=True) + EPS)
        q_ref[0, :, lo:lo + HEAD_DIM] = qn
        k_ref[0, :, lo:lo + HEAD_DIM] = kn
    v_ref[0] = c[:, 2 * D_DN:]

    sm = sm_ref[0]
    lane = lax.broadcasted_iota(jnp.int32, sm.shape, 1)
    beta = jax.nn.sigmoid(sm)
    xs = sm + dtb_ref[...]
    softplus = jnp.maximum(xs, 0.0) + jnp.log1p(jnp.exp(-jnp.abs(xs)))
    g = jnp.where((lane >= 8) & (lane < 16), -jnp.exp(alog_ref[...]) * softplus, 0.0)

    r = lax.broadcasted_iota(jnp.int32, (ts, ts), 0)
    cc = lax.broadcasted_iota(jnp.int32, (ts, ts), 1)
    same = (r // CHUNK) == (cc // CHUNK)
    m_tot = jnp.where(same, 1.0, 0.0).astype(BF16)
    m_pre = jnp.where(same & (cc <= r), 1.0, 0.0).astype(BF16)
    m_suf = jnp.where(same & (cc >= r), 1.0, 0.0).astype(BF16)
    parts = _split3(g)
    pre = sum(_dot(m_pre, p) for p in parts)
    suf = sum(_dot(m_suf, p) for p in parts)
    tot = sum(_dot(m_tot, p) for p in parts)
    gc = jnp.where(lane < 12, pre, suf)
    eg = jnp.exp(gc)
    ekg = jnp.exp(tot - gc)
    egt = jnp.exp(tot)

    g8 = pltpu.roll(gc, LANES - 8, 1)
    cs = jnp.where(lane < 8, beta, 0.0)
    cs = jnp.where((lane >= 8) & (lane < 16), gc, cs)
    cs = jnp.where((lane >= 16) & (lane < 24), pltpu.roll(eg, 8, 1), cs)
    cs = jnp.where((lane >= 24) & (lane < 32), pltpu.roll(ekg, 16, 1), cs)
    cs = jnp.where((lane >= 32) & (lane < 40), pltpu.roll(egt, 24, 1), cs)
    cs_ref[0] = cs

    g8t = g8.T
    for n in range(ts // CHUNK):
        gt_ref[0, n] = g8t[0:8, n * CHUNK:(n + 1) * CHUNK]


def _prep(pa, qkv, sm, conv_a, conv_qkv, alog, dtb):
    b, s, _ = pa.shape
    ts = TS_PREP
    nj = s // ts
    r8 = ts // 8
    n8 = s // 8

    def cur(w):
        return pl.BlockSpec((1, ts, w), lambda i, j: (i, j, 0))

    def prev(w):
        return pl.BlockSpec((1, 8, w), lambda i, j: (i, jnp.maximum(j * r8 - 1, 0), 0))

    def nxt(w):
        return pl.BlockSpec((1, 8, w), lambda i, j: (i, jnp.minimum((j + 1) * r8, n8 - 1), 0))

    return pl.pallas_call(
        _prep_kernel,
        grid=(b, nj),
        in_specs=[cur(3 * D_CONV), prev(3 * D_CONV), nxt(3 * D_CONV),
                  cur(3 * D_DN), prev(3 * D_DN), nxt(3 * D_DN), cur(LANES),
                  _resident(conv_a.shape), _resident(conv_qkv.shape),
                  _resident(alog.shape), _resident(dtb.shape)],
        out_specs=[cur(D_CONV), cur(D_DN), cur(D_DN), cur(D_DN), cur(LANES),
                   pl.BlockSpec((1, ts // CHUNK, 8, CHUNK), lambda i, j: (i, j, 0, 0))],
        out_shape=[jax.ShapeDtypeStruct((b, s, D_CONV), BF16),
                   jax.ShapeDtypeStruct((b, s, D_DN), F32),
                   jax.ShapeDtypeStruct((b, s, D_DN), F32),
                   jax.ShapeDtypeStruct((b, s, D_DN), F32),
                   jax.ShapeDtypeStruct((b, s, LANES), F32),
                   jax.ShapeDtypeStruct((b, s // CHUNK, 8, CHUNK), F32)],
        compiler_params=pltpu.CompilerParams(dimension_semantics=("arbitrary", "arbitrary"),
                                             vmem_limit_bytes=VMEM_LIMIT),
        name="prep",
    )(pa, pa, pa, qkv, qkv, qkv, sm, conv_a, conv_qkv, alog, dtb)


PAIR = 2 * HEAD_DIM


def _block_diag(x, half):
    lane = lax.broadcasted_iota(jnp.int32, x.shape, 1)
    zero = jnp.zeros_like(x)
    return jnp.concatenate([jnp.where(lane < half, x, zero), jnp.where(lane >= half, x, zero)], axis=0)


def _delta_prep(items):
    r = lax.broadcasted_iota(jnp.int32, (CHUNK, 2 * CHUNK), 0)
    lane = lax.broadcasted_iota(jnp.int32, (CHUNK, 2 * CHUNK), 1)
    c = jnp.where(lane < CHUNK, lane, lane - CHUNK)
    first = lane < CHUNK
    eye = jnp.where(r == c, 1.0, 0.0)

    def col256(cs, base, c0):
        return jnp.concatenate([jnp.broadcast_to(cs[:, base + c0:base + c0 + 1], (CHUNK, HEAD_DIM)),
                                jnp.broadcast_to(cs[:, base + c0 + 1:base + c0 + 2], (CHUNK, HEAD_DIM))], axis=1)

    for it in items:
        cs, c0 = it["cs"], it["c0"]
        it["beta"] = col256(cs, CS_BETA, c0)
        it["eg"] = col256(cs, CS_EG, c0)
        it["kb"] = it["k"] * it["beta"]
        lhs = jnp.concatenate([it["kb"], it["q"]], axis=0).astype(BF16)
        it["kq"] = lax.dot_general(lhs, _block_diag(it["k"].astype(BF16), HEAD_DIM),
                                   (((1,), (1,)), ((), ())), preferred_element_type=F32)

    for it in items:
        cs, c0 = it["cs"], it["c0"]
        incl = (r >= c) if it["lower"] else (r <= c)
        strict = (r > c) if it["lower"] else (r < c)
        gcol = jnp.where(first, cs[:, CS_G + c0:CS_G + c0 + 1], cs[:, CS_G + c0 + 1:CS_G + c0 + 2])
        decay = jnp.where(incl, jnp.exp(jnp.where(incl, gcol - it["grow"], 0.0)), 0.0)
        a = jnp.where(strict, it["kq"][:CHUNK] * decay, 0.0)
        it["intra"] = (it["kq"][CHUNK:] * decay).astype(BF16)
        it["p"] = eye - a
        it["a"] = a.astype(BF16)

    for it in items:
        it["y"] = _dot(it["a"], _block_diag(it["a"], CHUNK)).astype(BF16)
    steps = 2
    while 2 * steps < CHUNK:
        steps *= 2
        for it in items:
            res = _dot(jnp.concatenate([it["p"].astype(BF16), it["y"]], axis=0), _block_diag(it["y"], CHUNK))
            it["p"] = it["p"] + res[:CHUNK]
            it["y"] = res[CHUNK:].astype(BF16)
    for it in items:
        it["p"] = it["p"] + _dot(it["p"].astype(BF16), _block_diag(it["y"], CHUNK))

    for it in items:
        vb = (it["v"] * it["beta"]).astype(BF16)
        kbg = (it["kb"] * it["eg"]).astype(BF16)
        rhs = jnp.concatenate([_block_diag(vb, HEAD_DIM), _block_diag(kbg, HEAD_DIM)], axis=1)
        uw = _dot(it["p"].astype(BF16), rhs)
        it["u"] = uw[:, :PAIR]
        it["w"] = uw[:, PAIR:]
        it["qg"] = it["q"] * it["eg"]
        kg = (it["k"] * col256(it["cs"], CS_EKG, it["c0"])).astype(BF16)
        it["kgs"] = jnp.concatenate([kg[:, :HEAD_DIM], kg[:, HEAD_DIM:]], axis=0)


def _delta_scan(items, s_ref):
    for it in items:
        it["state"] = s_ref[it["sidx"]]
        sb = it["state"].astype(BF16)
        ws = []
        for h in range(2):
            sl = slice(h * HEAD_DIM, (h + 1) * HEAD_DIM)
            lhs = jnp.concatenate([it["w"][:, sl], it["qg"][:, sl]], axis=0).astype(BF16)
            ws.append(_dot(lhs, sb[:, sl]))
        it["ws_top"] = jnp.concatenate([ws[0][:CHUNK], ws[1][:CHUNK]], axis=1)
        it["ws_bot"] = jnp.concatenate([ws[0][CHUNK:], ws[1][CHUNK:]], axis=1)

    outs = []
    for it in items:
        cs, c0 = it["cs"], it["c0"]
        v_new = _block_diag((it["u"] - it["ws_top"]).astype(BF16), HEAD_DIM)
        outs.append(it["ws_bot"] + _dot(it["intra"], v_new))
        upd = lax.dot_general(it["kgs"], v_new, (((0,), (0,)), ((), ())), preferred_element_type=F32)
        egt = jnp.concatenate([jnp.broadcast_to(cs[0:1, CS_EGT + c0:CS_EGT + c0 + 1], (1, HEAD_DIM)),
                               jnp.broadcast_to(cs[0:1, CS_EGT + c0 + 1:CS_EGT + c0 + 2], (1, HEAD_DIM))], axis=1)
        s_ref[it["sidx"]] = it["state"] * egt + upd
    return outs


def _delta_kernel(qf_ref, kf_ref, vf_ref, csf_ref, gtf_ref, qb_ref, kb_ref, vb_ref, csb_ref, gtb_ref,
                  of_ref, ob_ref, s_ref):
    @pl.when(pl.program_id(0) == 0)
    def _():
        s_ref[...] = jnp.zeros_like(s_ref)

    nbatch = qf_ref.shape[0]
    nc = TC_DELTA // CHUNK
    npairs = N_HEADS // 2

    def body(i, carry):
        rounds = []
        for cc in range(DELTA_CHUNKS_PER_ITER):
            jf = DELTA_CHUNKS_PER_ITER * i + cc
            jb = nc - 1 - jf
            rf = pl.multiple_of(jf * CHUNK, CHUNK)
            rb = pl.multiple_of(jb * CHUNK, CHUNK)
            items = []
            for bi in range(nbatch):
                for lower, j, rows, (q_ref, k_ref, v_ref, cs_ref, gt_ref, o_ref) in (
                        (True, jf, rf, (qf_ref, kf_ref, vf_ref, csf_ref, gtf_ref, of_ref)),
                        (False, jb, rb, (qb_ref, kb_ref, vb_ref, csb_ref, gtb_ref, ob_ref))):
                    d = 0 if lower else 1
                    cs = cs_ref[bi, pl.ds(rows, CHUNK), :]
                    gt = gt_ref[bi, j]
                    for hp in range(npairs):
                        sl = slice(hp * PAIR, (hp + 1) * PAIR)
                        items.append(dict(
                            q=q_ref[bi, pl.ds(rows, CHUNK), sl], k=k_ref[bi, pl.ds(rows, CHUNK), sl],
                            v=v_ref[bi, pl.ds(rows, CHUNK), sl], cs=cs,
                            grow=gt[d * npairs + hp:d * npairs + hp + 1, :], c0=d * N_HEADS + 2 * hp,
                            sidx=(bi * 2 + d) * npairs + hp, lower=lower,
                            dst=(o_ref, bi, rows, sl)))
            rounds.append(items)
        _delta_prep([it for items in rounds for it in items])
        for items in rounds:
            outs = _delta_scan(items, s_ref)
            for it, o in zip(items, outs):
                o_ref, bi, rows, sl = it["dst"]
                o_ref[bi, pl.ds(rows, CHUNK), sl] = o
        return carry

    lax.fori_loop(0, nc // DELTA_CHUNKS_PER_ITER, body, 0)


def _delta(q, k, v, cs, gt):
    b, s, _ = q.shape
    tc = TC_DELTA
    nb = s // tc
    ncb = tc // CHUNK
    fwd = lambda w: pl.BlockSpec((b, tc, w), lambda j: (0, j, 0))
    bwd = lambda w: pl.BlockSpec((b, tc, w), lambda j: (0, nb - 1 - j, 0))
    gt_f = pl.BlockSpec((b, ncb, N_HEADS, 2 * CHUNK), lambda j: (0, j, 0, 0))
    gt_b = pl.BlockSpec((b, ncb, N_HEADS, 2 * CHUNK), lambda j: (0, nb - 1 - j, 0, 0))
    return pl.pallas_call(
        _delta_kernel,
        grid=(nb,),
        in_specs=[fwd(D_DN), fwd(D_DN), fwd(D_DN), fwd(LANES), gt_f,
                  bwd(D_DN), bwd(D_DN), bwd(D_DN), bwd(LANES), gt_b],
        out_specs=[fwd(D_DN), bwd(D_DN)],
        out_shape=[jax.ShapeDtypeStruct((b, s, D_DN), F32), jax.ShapeDtypeStruct((b, s, D_DN), F32)],
        scratch_shapes=[pltpu.VMEM((b * N_HEADS, HEAD_DIM, PAIR), F32)],
        compiler_params=pltpu.CompilerParams(dimension_semantics=("arbitrary",),
                                             vmem_limit_bytes=VMEM_LIMIT),
        name="delta",
    )(q, k, v, cs, gt, q, k, v, cs, gt)


def _out_kernel(x_ref, p_ref, of_ref, ob_ref, z_ref, ya_ref, gates_ref,
                gdn_ref, wa_ref, wdn_ref, wo_ref, gffn_ref, w1_ref, w2_ref,
                gple_ref, wpg_ref, wpp_ref, gfin_ref, out_ref):
    o = of_ref[...] + ob_ref[...]
    z = z_ref[...].astype(F32)
    gdn = gdn_ref[...]
    parts = []
    for h in range(N_HEADS):
        sl = slice(h * HEAD_DIM, (h + 1) * HEAD_DIM)
        parts.append(_rms(o[:, sl], gdn) * _silu(z[:, sl]))
    y_dn = jnp.concatenate(parts, axis=1).astype(BF16)
    gates = gates_ref[...].astype(F32)
    merged = (jax.nn.sigmoid(gates[:, :D_MODEL]) * _dot(ya_ref[...], wa_ref[...])
              + jax.nn.sigmoid(gates[:, D_MODEL:]) * _dot(y_dn, wdn_ref[...]))
    h1 = x_ref[...] + _dot(merged.astype(BF16), wo_ref[...])

    n2 = _rms(h1, gffn_ref[...]).astype(BF16)
    h2 = h1
    for c in range(D_FF // FF_CHUNK):
        sl = slice(c * FF_CHUNK, (c + 1) * FF_CHUNK)
        a = jnp.maximum(_dot(n2, w1_ref[:, sl]), 0.0)
        h2 = h2 + _dot((a * a).astype(BF16), w2_ref[sl, :])

    n3 = _rms(h2, gple_ref[...]).astype(BF16)
    gate_p = jax.nn.sigmoid(_dot(n3, wpg_ref[...]))
    h3 = h2 + gate_p * _dot(p_ref[...].astype(BF16), wpp_ref[...])
    out_ref[...] = _rms(h3, gfin_ref[...])


def _out(x2, p2, o_f, o_b, z, ya, gates, gdn, wa, wdn, wo, gffn, w1, w2, gple, wpg, wpp, gfin):
    t = x2.shape[0]
    tm = TM_OUT
    row = lambda w: pl.BlockSpec((tm, w), lambda i: (i, 0))
    weights = (gdn, wa, wdn, wo, gffn, w1, w2, gple, wpg, wpp, gfin)
    return pl.pallas_call(
        _out_kernel,
        grid=(t // tm,),
        in_specs=[row(D_MODEL), row(PLE_DIM), row(D_DN), row(D_DN), row(D_DN), row(D_CONV),
                  row(2 * D_MODEL)] + [_resident(w.shape) for w in weights],
        out_specs=row(D_MODEL),
        out_shape=jax.ShapeDtypeStruct((t, D_MODEL), F32),
        compiler_params=pltpu.CompilerParams(dimension_semantics=("arbitrary",),
                                             vmem_limit_bytes=VMEM_LIMIT),
        name="out",
    )(x2, p2, o_f, o_b, z, ya, gates, *weights)


def kernel(x, p, g_mix, w_in, conv_a, conv_qkv, a_log_f, a_log_b, dt_bias_f, dt_bias_b, g_dn_norm,
           w_a_out, w_dn_out, w_o, g_ffn, w_ff1, w_ff2, g_ple, w_ple_gate, w_ple_proj, g_final):
    b, s, d = x.shape
    depth = p.shape[0]
    assert depth == 1 and d == D_MODEL and s % TC_DELTA == 0 and s % TS_PREP == 0
    t = b * s
    i = 0

    w = w_in[i]
    c0 = 3 * D_CONV
    c1 = c0 + 3 * D_DN
    c2 = c1 + D_DN
    c3 = c2 + 4 * N_HEADS
    wpa = w[:, :c0].astype(BF16)
    wqkv = w[:, c0:c1].astype(BF16)
    wz = w[:, c1:c2].astype(BF16)
    wsm = jnp.pad(w[:, c2:c3], ((0, 0), (0, LANES - 4 * N_HEADS))).astype(BF16)
    wg = w[:, c3:].astype(BF16)

    x2 = x.reshape(t, d)
    pa, qkv, z, sm, gates = _proj(x2, g_mix[i][None, :], wpa, wqkv, wz, wsm, wg)

    zeros8 = jnp.zeros((2 * N_HEADS,), F32)
    tail = jnp.zeros((LANES - 4 * N_HEADS,), F32)
    alog = jnp.concatenate([zeros8, a_log_f[i], a_log_b[i], tail])[None, :]
    dtb = jnp.concatenate([zeros8, dt_bias_f[i], dt_bias_b[i], tail])[None, :]
    ya, q, k, v, cs, gt = _prep(pa.reshape(b, s, -1), qkv.reshape(b, s, -1), sm.reshape(b, s, -1),
                                conv_a[i], conv_qkv[i], alog, dtb)

    gt = gt.reshape(b, s // CHUNK, N_HEADS, 2 * CHUNK)
    o_f, o_b = _delta(q, k, v, cs, gt)

    out = _out(x2, p[i].reshape(t, PLE_DIM), o_f.reshape(t, D_DN), o_b.reshape(t, D_DN), z,
               ya.reshape(t, D_CONV), gates,
               g_dn_norm[i][None, :], w_a_out[i].astype(BF16), w_dn_out[i].astype(BF16),
               w_o[i].astype(BF16), g_ffn[i][None, :], w_ff1[i].astype(BF16), w_ff2[i].astype(BF16),
               g_ple[i][None, :], w_ple_gate[i].astype(BF16), w_ple_proj[i].astype(BF16),
               g_final[None, :])
    return out.reshape(b, s, d)
```

```python
import functools

import jax
import jax.numpy as jnp
from jax import lax
from jax.experimental import pallas as pl
from jax.experimental.pallas import tpu as pltpu

F32 = jnp.float32
BF16 = jnp.bfloat16

D_MODEL = 1024
PLE_DIM = 256
D_CONV = 512
N_HEADS = 4
HEAD_DIM = 128
D_DN = N_HEADS * HEAD_DIM
CHUNK = 64
D_FF = 4 * D_MODEL
EPS = 1e-6

LANES = 128
VMEM_LIMIT = 56 * 1024 * 1024

TS_FRONT = 512
TC_DELTA = 256
DELTA_CHUNKS_PER_ITER = 2
TM_OUT = 512
FF_CHUNK = 1024

CS_BETA, CS_G, CS_EG, CS_EKG, CS_EGT = 0, 8, 16, 24, 32


def _resident(shape):
    nd = len(shape)
    return pl.BlockSpec(shape, lambda *_: (0,) * nd, pipeline_mode=pl.Buffered(1))


def _rms(x, g):
    return x * lax.rsqrt(jnp.mean(x * x, axis=-1, keepdims=True) + EPS) * g


def _dot(a, b):
    return jnp.dot(a, b, preferred_element_type=F32)


def _silu(x):
    return x * jax.nn.sigmoid(x)


HALO = 8


def _conv3(ext, w, ts):
    n = ext.shape[0]
    up = pltpu.roll(ext, 1, 0)[HALO:HALO + ts]
    dn = pltpu.roll(ext, n - 1, 0)[HALO:HALO + ts]
    return up * w[0:1, :] + ext[HALO:HALO + ts] * w[1:2, :] + dn * w[2:3, :]


def _chunk_prefix_sums(g):
    ts = g.shape[0]
    hi = g.astype(BF16).astype(F32)
    r1 = g - hi
    mid = r1.astype(BF16).astype(F32)
    lo = (r1 - mid).astype(BF16).astype(F32)
    packed = (hi + pltpu.roll(mid, 16, 1) + pltpu.roll(lo, 32, 1)).astype(BF16)
    r = lax.broadcasted_iota(jnp.int32, (CHUNK, CHUNK), 0)
    c = lax.broadcasted_iota(jnp.int32, (CHUNK, CHUNK), 1)
    tri = jnp.where(c <= r, 1.0, 0.0).astype(BF16)
    res = jnp.concatenate([_dot(tri, packed[n * CHUNK:(n + 1) * CHUNK]) for n in range(ts // CHUNK)], axis=0)
    return res + pltpu.roll(res, LANES - 16, 1) + pltpu.roll(res, LANES - 32, 1)


def _front_kernel(x_ref, xp_ref, xn_ref, gmix_ref, wpa_ref, wqkv_ref, wz_ref, wsm_ref, wg_ref,
                  conv_a_ref, conv_qkv_ref, alog_ref, dtb_ref,
                  ya_ref, q_ref, k_ref, v_ref, cs_ref, gt_ref, z_ref, gates_ref):
    j = pl.program_id(1)
    nj = pl.num_programs(1)
    ts = TS_FRONT
    has_prev = (j > 0).astype(F32)
    has_next = (j < nj - 1).astype(F32)
    inner = slice(HALO, HALO + ts)

    gm = gmix_ref[...]
    n_cur = _rms(x_ref[0], gm)
    n_ext = jnp.concatenate([_rms(xp_ref[0], gm) * has_prev, n_cur, _rms(xn_ref[0], gm) * has_next],
                            axis=0).astype(BF16)
    n_b = n_cur.astype(BF16)

    pa = _dot(n_ext, wpa_ref[...])
    u = pa[:, :D_CONV] * pa[:, 2 * D_CONV:]
    ya = pa[inner, D_CONV:2 * D_CONV] * _conv3(u, conv_a_ref[...], ts)
    ya_ref[0] = ya.astype(ya_ref.dtype)

    for gi, (ref, scale) in enumerate(((q_ref, HEAD_DIM ** -0.5), (k_ref, None), (v_ref, None))):
        sl = slice(gi * D_DN, (gi + 1) * D_DN)
        c = _silu(_conv3(_dot(n_ext, wqkv_ref[:, sl]), conv_qkv_ref[:, sl], ts))
        if ref is v_ref:
            v_ref[0] = c
            continue
        for h in range(N_HEADS):
            hs = slice(h * HEAD_DIM, (h + 1) * HEAD_DIM)
            ch = c[:, hs]
            cn = ch * lax.rsqrt(jnp.sum(ch * ch, axis=-1, keepdims=True) + EPS)
            ref[0, :, hs] = cn if scale is None else cn * scale

    z_ref[0] = _dot(n_b, wz_ref[...]).astype(z_ref.dtype)
    gates_ref[0] = _dot(n_b, wg_ref[...]).astype(gates_ref.dtype)

    sm = _dot(n_b, wsm_ref[...])
    lane = lax.broadcasted_iota(jnp.int32, sm.shape, 1)
    beta = jax.nn.sigmoid(sm)
    xs = sm + dtb_ref[...]
    softplus = jnp.maximum(xs, 0.0) + jnp.log1p(jnp.exp(-jnp.abs(xs)))
    g = jnp.where((lane >= 8) & (lane < 16), -jnp.exp(alog_ref[...]) * softplus, 0.0)

    pre = _chunk_prefix_sums(g)
    pre3 = pre.reshape(ts // CHUNK, CHUNK, LANES)
    tot = jnp.broadcast_to(pre3[:, CHUNK - 1:CHUNK, :], pre3.shape).reshape(ts, LANES)
    gc = jnp.where(lane < 12, pre, tot - pre + g)
    eg = jnp.exp(gc)
    ekg = jnp.exp(tot - gc)
    egt = jnp.exp(tot)

    g8 = pltpu.roll(gc, LANES - 8, 1)
    cs = jnp.where(lane < 8, beta, 0.0)
    cs = jnp.where((lane >= 8) & (lane < 16), gc, cs)
    cs = jnp.where((lane >= 16) & (lane < 24), pltpu.roll(eg, 8, 1), cs)
    cs = jnp.where((lane >= 24) & (lane < 32), pltpu.roll(ekg, 16, 1), cs)
    cs = jnp.where((lane >= 32) & (lane < 40), pltpu.roll(egt, 24, 1), cs)
    cs_ref[0] = cs

    g8t = g8.T
    for n in range(ts // CHUNK):
        gt_ref[0, n] = g8t[0:8, n * CHUNK:(n + 1) * CHUNK]


def _front(x, g_mix, wpa, wqkv, wz, wsm, wg, conv_a, conv_qkv, alog, dtb):
    b, s, d = x.shape
    ts = TS_FRONT
    nj = s // ts
    r8 = ts // HALO
    n8 = s // HALO

    def cur(w):
        return pl.BlockSpec((1, ts, w), lambda i, j: (i, j, 0))

    prev = pl.BlockSpec((1, HALO, d), lambda i, j: (i, jnp.maximum(j * r8 - 1, 0), 0))
    nxt = pl.BlockSpec((1, HALO, d), lambda i, j: (i, jnp.minimum((j + 1) * r8, n8 - 1), 0))
    consts = (g_mix, wpa, wqkv, wz, wsm, wg, conv_a, conv_qkv, alog, dtb)
    return pl.pallas_call(
        _front_kernel,
        grid=(b, nj),
        in_specs=[cur(d), prev, nxt] + [_resident(c.shape) for c in consts],
        out_specs=[cur(D_CONV), cur(D_DN), cur(D_DN), cur(D_DN), cur(LANES),
                   pl.BlockSpec((1, ts // CHUNK, 8, CHUNK), lambda i, j: (i, j, 0, 0)),
                   cur(D_DN), cur(2 * D_MODEL)],
        out_shape=[jax.ShapeDtypeStruct((b, s, D_CONV), BF16),
                   jax.ShapeDtypeStruct((b, s, D_DN), F32),
                   jax.ShapeDtypeStruct((b, s, D_DN), F32),
                   jax.ShapeDtypeStruct((b, s, D_DN), F32),
                   jax.ShapeDtypeStruct((b, s, LANES), F32),
                   jax.ShapeDtypeStruct((b, s // CHUNK, 8, CHUNK), F32),
                   jax.ShapeDtypeStruct((b, s, D_DN), BF16),
                   jax.ShapeDtypeStruct((b, s, 2 * D_MODEL), BF16)],
        compiler_params=pltpu.CompilerParams(dimension_semantics=("arbitrary", "arbitrary"),
                                             vmem_limit_bytes=VMEM_LIMIT),
        name="front",
    )(x, x, x, *consts)


PAIR = 2 * HEAD_DIM


def _block_diag(x, half):
    lane = lax.broadcasted_iota(jnp.int32, x.shape, 1)
    zero = jnp.zeros_like(x)
    return jnp.concatenate([jnp.where(lane < half, x, zero), jnp.where(lane >= half, x, zero)], axis=0)


def _delta_prep(items):
    r = lax.broadcasted_iota(jnp.int32, (CHUNK, 2 * CHUNK), 0)
    lane = lax.broadcasted_iota(jnp.int32, (CHUNK, 2 * CHUNK), 1)
    c = jnp.where(lane < CHUNK, lane, lane - CHUNK)
    first = lane < CHUNK
    eye = jnp.where(r == c, 1.0, 0.0)

    def col256(cs, base, c0):
        return jnp.concatenate([jnp.broadcast_to(cs[:, base + c0:base + c0 + 1], (CHUNK, HEAD_DIM)),
                                jnp.broadcast_to(cs[:, base + c0 + 1:base + c0 + 2], (CHUNK, HEAD_DIM))], axis=1)

    for it in items:
        cs, c0 = it["cs"], it["c0"]
        it["beta"] = col256(cs, CS_BETA, c0)
        it["eg"] = col256(cs, CS_EG, c0)
        it["kb"] = it["k"] * it["beta"]
        lhs = jnp.concatenate([it["kb"], it["q"]], axis=0).astype(BF16)
        it["kq"] = lax.dot_general(lhs, _block_diag(it["k"].astype(BF16), HEAD_DIM),
                                   (((1,), (1,)), ((), ())), preferred_element_type=F32)

    for it in items:
        cs, c0 = it["cs"], it["c0"]
        incl = (r >= c) if it["lower"] else (r <= c)
        strict = (r > c) if it["lower"] else (r < c)
        gcol = jnp.where(first, cs[:, CS_G + c0:CS_G + c0 + 1], cs[:, CS_G + c0 + 1:CS_G + c0 + 2])
        decay = jnp.where(incl, jnp.exp(jnp.where(incl, gcol - it["grow"], 0.0)), 0.0)
        a = jnp.where(strict, it["kq"][:CHUNK] * decay, 0.0)
        it["intra"] = (it["kq"][CHUNK:] * decay).astype(BF16)
        it["p"] = eye - a
        it["a"] = a.astype(BF16)

    for it in items:
        it["y"] = _dot(it["a"], _block_diag(it["a"], CHUNK)).astype(BF16)
    steps = 2
    while 2 * steps < CHUNK:
        steps *= 2
        for it in items:
            res = _dot(jnp.concatenate([it["p"].astype(BF16), it["y"]], axis=0), _block_diag(it["y"], CHUNK))
            it["p"] = it["p"] + res[:CHUNK]
            it["y"] = res[CHUNK:].astype(BF16)
    for it in items:
        it["p"] = it["p"] + _dot(it["p"].astype(BF16), _block_diag(it["y"], CHUNK))

    for it in items:
        vb = (it["v"] * it["beta"]).astype(BF16)
        kbg = (it["kb"] * it["eg"]).astype(BF16)
        rhs = jnp.concatenate([_block_diag(vb, HEAD_DIM), _block_diag(kbg, HEAD_DIM)], axis=1)
        uw = _dot(it["p"].astype(BF16), rhs)
        it["u"] = uw[:, :PAIR]
        it["w"] = uw[:, PAIR:]
        it["qg"] = it["q"] * it["eg"]
        kg = (it["k"] * col256(it["cs"], CS_EKG, it["c0"])).astype(BF16)
        it["kgs"] = jnp.concatenate([kg[:, :HEAD_DIM], kg[:, HEAD_DIM:]], axis=0)


def _delta_scan(items, s_ref):
    for it in items:
        it["state"] = s_ref[it["sidx"]]
        sb = it["state"].astype(BF16)
        ws = []
        for h in range(2):
            sl = slice(h * HEAD_DIM, (h + 1) * HEAD_DIM)
            lhs = jnp.concatenate([it["w"][:, sl], it["qg"][:, sl]], axis=0).astype(BF16)
            ws.append(_dot(lhs, sb[:, sl]))
        it["ws_top"] = jnp.concatenate([ws[0][:CHUNK], ws[1][:CHUNK]], axis=1)
        it["ws_bot"] = jnp.concatenate([ws[0][CHUNK:], ws[1][CHUNK:]], axis=1)

    outs = []
    for it in items:
        cs, c0 = it["cs"], it["c0"]
        v_new = _block_diag((it["u"] - it["ws_top"]).astype(BF16), HEAD_DIM)
        outs.append(it["ws_bot"] + _dot(it["intra"], v_new))
        upd = lax.dot_general(it["kgs"], v_new, (((0,), (0,)), ((), ())), preferred_element_type=F32)
        egt = jnp.concatenate([jnp.broadcast_to(cs[0:1, CS_EGT + c0:CS_EGT + c0 + 1], (1, HEAD_DIM)),
                               jnp.broadcast_to(cs[0:1, CS_EGT + c0 + 1:CS_EGT + c0 + 2], (1, HEAD_DIM))], axis=1)
        s_ref[it["sidx"]] = it["state"] * egt + upd
    return outs


def _delta_kernel(qf_ref, kf_ref, vf_ref, csf_ref, gtf_ref, qb_ref, kb_ref, vb_ref, csb_ref, gtb_ref,
                  of_ref, ob_ref, s_ref):
    @pl.when(pl.program_id(0) == 0)
    def _():
        s_ref[...] = jnp.zeros_like(s_ref)

    nbatch = qf_ref.shape[0]
    nc = TC_DELTA // CHUNK
    npairs = N_HEADS // 2

    def body(i, carry):
        rounds = []
        for cc in range(DELTA_CHUNKS_PER_ITER):
            jf = DELTA_CHUNKS_PER_ITER * i + cc
            jb = nc - 1 - jf
            rf = pl.multiple_of(jf * CHUNK, CHUNK)
            rb = pl.multiple_of(jb * CHUNK, CHUNK)
            items = []
            for bi in range(nbatch):
                for lower, j, rows, (q_ref, k_ref, v_ref, cs_ref, gt_ref, o_ref) in (
                        (True, jf, rf, (qf_ref, kf_ref, vf_ref, csf_ref, gtf_ref, of_ref)),
                        (False, jb, rb, (qb_ref, kb_ref, vb_ref, csb_ref, gtb_ref, ob_ref))):
                    d = 0 if lower else 1
                    cs = cs_ref[bi, pl.ds(rows, CHUNK), :]
                    gt = gt_ref[bi, j]
                    for hp in range(npairs):
                        sl = slice(hp * PAIR, (hp + 1) * PAIR)
                        items.append(dict(
                            q=q_ref[bi, pl.ds(rows, CHUNK), sl], k=k_ref[bi, pl.ds(rows, CHUNK), sl],
                            v=v_ref[bi, pl.ds(rows, CHUNK), sl], cs=cs,
                            grow=gt[d * npairs + hp:d * npairs + hp + 1, :], c0=d * N_HEADS + 2 * hp,
                            sidx=(bi * 2 + d) * npairs + hp, lower=lower,
                            dst=(o_ref, bi, rows, sl)))
            rounds.append(items)
        _delta_prep([it for items in rounds for it in items])
        for items in rounds:
            outs = _delta_scan(items, s_ref)
            for it, o in zip(items, outs):
                o_ref, bi, rows, sl = it["dst"]
                o_ref[bi, pl.ds(rows, CHUNK), sl] = o
        return carry

    lax.fori_loop(0, nc // DELTA_CHUNKS_PER_ITER, body, 0)


def _delta(q, k, v, cs, gt):
    b, s, _ = q.shape
    tc = TC_DELTA
    nb = s // tc
    ncb = tc // CHUNK
    fwd = lambda w: pl.BlockSpec((b, tc, w), lambda j: (0, j, 0))
    bwd = lambda w: pl.BlockSpec((b, tc, w), lambda j: (0, nb - 1 - j, 0))
    gt_f = pl.BlockSpec((b, ncb, N_HEADS, 2 * CHUNK), lambda j: (0, j, 0, 0))
    gt_b = pl.BlockSpec((b, ncb, N_HEADS, 2 * CHUNK), lambda j: (0, nb - 1 - j, 0, 0))
    return pl.pallas_call(
        _delta_kernel,
        grid=(nb,),
        in_specs=[fwd(D_DN), fwd(D_DN), fwd(D_DN), fwd(LANES), gt_f,
                  bwd(D_DN), bwd(D_DN), bwd(D_DN), bwd(LANES), gt_b],
        out_specs=[fwd(D_DN), bwd(D_DN)],
        out_shape=[jax.ShapeDtypeStruct((b, s, D_DN), F32), jax.ShapeDtypeStruct((b, s, D_DN), F32)],
        scratch_shapes=[pltpu.VMEM((b * N_HEADS, HEAD_DIM, PAIR), F32)],
        compiler_params=pltpu.CompilerParams(dimension_semantics=("arbitrary",),
                                             vmem_limit_bytes=VMEM_LIMIT),
        name="delta",
    )(q, k, v, cs, gt, q, k, v, cs, gt)


def _out_kernel(x_ref, p_ref, of_ref, ob_ref, z_ref, ya_ref, gates_ref,
                gdn_ref, wa_ref, wdn_ref, wo_ref, gffn_ref, w1_ref, w2_ref,
                gple_ref, wpg_ref, wpp_ref, gfin_ref, out_ref):
    o = of_ref[...] + ob_ref[...]
    z = z_ref[...].astype(F32)
    gdn = gdn_ref[...]
    parts = []
    for h in range(N_HEADS):
        sl = slice(h * HEAD_DIM, (h + 1) * HEAD_DIM)
        parts.append(_rms(o[:, sl], gdn) * _silu(z[:, sl]))
    y_dn = jnp.concatenate(parts, axis=1).astype(BF16)
    gates = gates_ref[...].astype(F32)
    merged = (jax.nn.sigmoid(gates[:, :D_MODEL]) * _dot(ya_ref[...], wa_ref[...])
              + jax.nn.sigmoid(gates[:, D_MODEL:]) * _dot(y_dn, wdn_ref[...]))
    h1 = x_ref[...] + _dot(merged.astype(BF16), wo_ref[...])

    n2 = _rms(h1, gffn_ref[...]).astype(BF16)
    h2 = h1
    for c in range(D_FF // FF_CHUNK):
        sl = slice(c * FF_CHUNK, (c + 1) * FF_CHUNK)
        a = jnp.maximum(_dot(n2, w1_ref[:, sl]), 0.0)
        h2 = h2 + _dot((a * a).astype(BF16), w2_ref[sl, :])

    n3 = _rms(h2, gple_ref[...]).astype(BF16)
    gate_p = jax.nn.sigmoid(_dot(n3, wpg_ref[...]))
    h3 = h2 + gate_p * _dot(p_ref[...].astype(BF16), wpp_ref[...])
    out_ref[...] = _rms(h3, gfin_ref[...])


def _out(x2, p2, o_f, o_b, z, ya, gates, gdn, wa, wdn, wo, gffn, w1, w2, gple, wpg, wpp, gfin):
    t = x2.shape[0]
    tm = TM_OUT
    row = lambda w: pl.BlockSpec((tm, w), lambda i: (i, 0))
    weights = (gdn, wa, wdn, wo, gffn, w1, w2, gple, wpg, wpp, gfin)
    return pl.pallas_call(
        _out_kernel,
        grid=(t // tm,),
        in_specs=[row(D_MODEL), row(PLE_DIM), row(D_DN), row(D_DN), row(D_DN), row(D_CONV),
                  row(2 * D_MODEL)] + [_resident(w.shape) for w in weights],
        out_specs=row(D_MODEL),
        out_shape=jax.ShapeDtypeStruct((t, D_MODEL), F32),
        compiler_params=pltpu.CompilerParams(dimension_semantics=("arbitrary",),
                                             vmem_limit_bytes=VMEM_LIMIT),
        name="out",
    )(x2, p2, o_f, o_b, z, ya, gates, *weights)


def kernel(x, p, g_mix, w_in, conv_a, conv_qkv, a_log_f, a_log_b, dt_bias_f, dt_bias_b, g_dn_norm,
           w_a_out, w_dn_out, w_o, g_ffn, w_ff1, w_ff2, g_ple, w_ple_gate, w_ple_proj, g_final):
    b, s, d = x.shape
    depth = p.shape[0]
    assert depth == 1 and d == D_MODEL and s % TC_DELTA == 0 and s % TS_FRONT == 0
    t = b * s
    i = 0

    w = w_in[i]
    c0 = 3 * D_CONV
    c1 = c0 + 3 * D_DN
    c2 = c1 + D_DN
    c3 = c2 + 4 * N_HEADS
    wpa = w[:, :c0].astype(BF16)
    wqkv = w[:, c0:c1].astype(BF16)
    wz = w[:, c1:c2].astype(BF16)
    wsm = jnp.pad(w[:, c2:c3], ((0, 0), (0, LANES - 4 * N_HEADS))).astype(BF16)
    wg = w[:, c3:].astype(BF16)

    zeros8 = jnp.zeros((2 * N_HEADS,), F32)
    tail = jnp.zeros((LANES - 4 * N_HEADS,), F32)
    alog = jnp.concatenate([zeros8, a_log_f[i], a_log_b[i], tail])[None, :]
    dtb = jnp.concatenate([zeros8, dt_bias_f[i], dt_bias_b[i], tail])[None, :]
    ya, q, k, v, cs, gt, z, gates = _front(x, g_mix[i][None, :], wpa, wqkv, wz, wsm, wg,
                                           conv_a[i], conv_qkv[i], alog, dtb)

    gt = gt.reshape(b, s // CHUNK, N_HEADS, 2 * CHUNK)
    o_f, o_b = _delta(q, k, v, cs, gt)

    out = _out(x.reshape(t, d), p[i].reshape(t, PLE_DIM), o_f.reshape(t, D_DN), o_b.reshape(t, D_DN),
               z.reshape(t, D_DN), ya.reshape(t, D_CONV), gates.reshape(t, 2 * D_MODEL),
               g_dn_norm[i][None, :], w_a_out[i].astype(BF16), w_dn_out[i].astype(BF16),
               w_o[i].astype(BF16), g_ffn[i][None, :], w_ff1[i].astype(BF16), w_ff2[i].astype(BF16),
               g_ple[i][None, :], w_ple_gate[i].astype(BF16), w_ple_proj[i].astype(BF16),
               g_final[None, :])
    return out.reshape(b, s, d)
```

```python
import functools

import jax
import jax.numpy as jnp
from jax import lax
from jax.experimental import pallas as pl
from jax.experimental.pallas import tpu as pltpu

F32 = jnp.float32
BF16 = jnp.bfloat16

D_MODEL = 1024
PLE_DIM = 256
D_CONV = 512
N_HEADS = 4
HEAD_DIM = 128
D_DN = N_HEADS * HEAD_DIM
CHUNK = 64
D_FF = 4 * D_MODEL
EPS = 1e-6

LANES = 128
VMEM_LIMIT = 56 * 1024 * 1024

TS_FRONT = 512
TC_DELTA = 256
DELTA_CHUNKS_PER_ITER = 2
TM_OUT = 512
FF_CHUNK = 1024

CS_BETA, CS_G, CS_EG, CS_EKG, CS_EGT = 0, 8, 16, 24, 32


def _resident(shape):
    nd = len(shape)
    return pl.BlockSpec(shape, lambda *_: (0,) * nd, pipeline_mode=pl.Buffered(1))


def _rms(x, g):
    return x * lax.rsqrt(jnp.mean(x * x, axis=-1, keepdims=True) + EPS) * g


def _dot(a, b):
    return jnp.dot(a, b, preferred_element_type=F32)


def _silu(x):
    return x * jax.nn.sigmoid(x)


HALO = 8


def _conv3(ext, w, ts):
    n = ext.shape[0]
    up = pltpu.roll(ext, 1, 0)[HALO:HALO + ts]
    dn = pltpu.roll(ext, n - 1, 0)[HALO:HALO + ts]
    return up * w[0:1, :] + ext[HALO:HALO + ts] * w[1:2, :] + dn * w[2:3, :]


def _chunk_prefix_sums(g):
    ts = g.shape[0]
    hi = g.astype(BF16).astype(F32)
    r1 = g - hi
    mid = r1.astype(BF16).astype(F32)
    lo = (r1 - mid).astype(BF16).astype(F32)
    packed = (hi + pltpu.roll(mid, 16, 1) + pltpu.roll(lo, 32, 1)).astype(BF16)
    r = lax.broadcasted_iota(jnp.int32, (CHUNK, CHUNK), 0)
    c = lax.broadcasted_iota(jnp.int32, (CHUNK, CHUNK), 1)
    tri = jnp.where(c <= r, 1.0, 0.0).astype(BF16)
    res = jnp.concatenate([_dot(tri, packed[n * CHUNK:(n + 1) * CHUNK]) for n in range(ts // CHUNK)], axis=0)
    return res + pltpu.roll(res, LANES - 16, 1) + pltpu.roll(res, LANES - 32, 1)


def _front_kernel(x_ref, xp_ref, xn_ref, gmix_ref, wpa_ref, wqkv_ref, wz_ref, wsm_ref, wg_ref,
                  conv_a_ref, conv_qkv_ref, alog_ref, dtb_ref,
                  ya_ref, q_ref, k_ref, v_ref, cs_ref, gt_ref, z_ref, gates_ref):
    j = pl.program_id(1)
    nj = pl.num_programs(1)
    ts = TS_FRONT
    has_prev = (j > 0).astype(F32)
    has_next = (j < nj - 1).astype(F32)
    inner = slice(HALO, HALO + ts)

    gm = gmix_ref[...]
    n_cur = _rms(x_ref[0], gm)
    n_ext = jnp.concatenate([_rms(xp_ref[0], gm) * has_prev, n_cur, _rms(xn_ref[0], gm) * has_next],
                            axis=0).astype(BF16)
    n_b = n_cur.astype(BF16)

    sm = _dot(n_b, wsm_ref[...])
    lane = lax.broadcasted_iota(jnp.int32, sm.shape, 1)
    beta = jax.nn.sigmoid(sm)
    xs = sm + dtb_ref[...]
    softplus = jnp.maximum(xs, 0.0) + jnp.log1p(jnp.exp(-jnp.abs(xs)))
    g = jnp.where((lane >= 8) & (lane < 16), -jnp.exp(alog_ref[...]) * softplus, 0.0)

    pa = _dot(n_ext, wpa_ref[...])
    u = pa[:, :D_CONV] * pa[:, 2 * D_CONV:]
    ya = pa[inner, D_CONV:2 * D_CONV] * _conv3(u, conv_a_ref[...], ts)
    ya_ref[0] = ya.astype(ya_ref.dtype)

    pre = _chunk_prefix_sums(g)
    pre3 = pre.reshape(ts // CHUNK, CHUNK, LANES)
    tot = jnp.broadcast_to(pre3[:, CHUNK - 1:CHUNK, :], pre3.shape).reshape(ts, LANES)
    gc = jnp.where(lane < 12, pre, tot - pre + g)
    eg = jnp.exp(gc)
    ekg = jnp.exp(tot - gc)
    egt = jnp.exp(tot)

    g8 = pltpu.roll(gc, LANES - 8, 1)
    cs = jnp.where(lane < 8, beta, 0.0)
    cs = jnp.where((lane >= 8) & (lane < 16), gc, cs)
    cs = jnp.where((lane >= 16) & (lane < 24), pltpu.roll(eg, 8, 1), cs)
    cs = jnp.where((lane >= 24) & (lane < 32), pltpu.roll(ekg, 16, 1), cs)
    cs = jnp.where((lane >= 32) & (lane < 40), pltpu.roll(egt, 24, 1), cs)
    cs_ref[0] = cs

    g8t = g8.T
    for n in range(ts // CHUNK):
        gt_ref[0, n] = g8t[0:8, n * CHUNK:(n + 1) * CHUNK]

    for gi, (ref, scale) in enumerate(((q_ref, HEAD_DIM ** -0.5), (k_ref, None), (v_ref, None))):
        sl = slice(gi * D_DN, (gi + 1) * D_DN)
        c = _silu(_conv3(_dot(n_ext, wqkv_ref[:, sl]), conv_qkv_ref[:, sl], ts))
        if ref is v_ref:
            v_ref[0] = c
            continue
        for h in range(N_HEADS):
            hs = slice(h * HEAD_DIM, (h + 1) * HEAD_DIM)
            ch = c[:, hs]
            cn = ch * lax.rsqrt(jnp.sum(ch * ch, axis=-1, keepdims=True) + EPS)
            ref[0, :, hs] = cn if scale is None else cn * scale

    z_ref[0] = _dot(n_b, wz_ref[...]).astype(z_ref.dtype)
    gates_ref[0] = _dot(n_b, wg_ref[...]).astype(gates_ref.dtype)


def _front(x, g_mix, wpa, wqkv, wz, wsm, wg, conv_a, conv_qkv, alog, dtb):
    b, s, d = x.shape
    ts = TS_FRONT
    nj = s // ts
    r8 = ts // HALO
    n8 = s // HALO

    def cur(w):
        return pl.BlockSpec((1, ts, w), lambda i, j: (i, j, 0))

    prev = pl.BlockSpec((1, HALO, d), lambda i, j: (i, jnp.maximum(j * r8 - 1, 0), 0))
    nxt = pl.BlockSpec((1, HALO, d), lambda i, j: (i, jnp.minimum((j + 1) * r8, n8 - 1), 0))
    consts = (g_mix, wpa, wqkv, wz, wsm, wg, conv_a, conv_qkv, alog, dtb)
    return pl.pallas_call(
        _front_kernel,
        grid=(b, nj),
        in_specs=[cur(d), prev, nxt] + [_resident(c.shape) for c in consts],
        out_specs=[cur(D_CONV), cur(D_DN), cur(D_DN), cur(D_DN), cur(LANES),
                   pl.BlockSpec((1, ts // CHUNK, 8, CHUNK), lambda i, j: (i, j, 0, 0)),
                   cur(D_DN), cur(2 * D_MODEL)],
        out_shape=[jax.ShapeDtypeStruct((b, s, D_CONV), BF16),
                   jax.ShapeDtypeStruct((b, s, D_DN), F32),
                   jax.ShapeDtypeStruct((b, s, D_DN), F32),
                   jax.ShapeDtypeStruct((b, s, D_DN), F32),
                   jax.ShapeDtypeStruct((b, s, LANES), F32),
                   jax.ShapeDtypeStruct((b, s // CHUNK, 8, CHUNK), F32),
                   jax.ShapeDtypeStruct((b, s, D_DN), BF16),
                   jax.ShapeDtypeStruct((b, s, 2 * D_MODEL), BF16)],
        compiler_params=pltpu.CompilerParams(dimension_semantics=("arbitrary", "arbitrary"),
                                             vmem_limit_bytes=VMEM_LIMIT),
        name="front",
    )(x, x, x, *consts)


PAIR = 2 * HEAD_DIM


def _block_diag(x, half):
    lane = lax.broadcasted_iota(jnp.int32, x.shape, 1)
    zero = jnp.zeros_like(x)
    return jnp.concatenate([jnp.where(lane < half, x, zero), jnp.where(lane >= half, x, zero)], axis=0)


def _delta_prep(items):
    r = lax.broadcasted_iota(jnp.int32, (CHUNK, 2 * CHUNK), 0)
    lane = lax.broadcasted_iota(jnp.int32, (CHUNK, 2 * CHUNK), 1)
    c = jnp.where(lane < CHUNK, lane, lane - CHUNK)
    first = lane < CHUNK
    eye = jnp.where(r == c, 1.0, 0.0)

    def col256(cs, base, c0):
        return jnp.concatenate([jnp.broadcast_to(cs[:, base + c0:base + c0 + 1], (CHUNK, HEAD_DIM)),
                                jnp.broadcast_to(cs[:, base + c0 + 1:base + c0 + 2], (CHUNK, HEAD_DIM))], axis=1)

    for it in items:
        cs, c0 = it["cs"], it["c0"]
        it["beta"] = col256(cs, CS_BETA, c0)
        it["eg"] = col256(cs, CS_EG, c0)
        it["kb"] = it["k"] * it["beta"]
        lhs = jnp.concatenate([it["kb"], it["q"]], axis=0).astype(BF16)
        it["kq"] = lax.dot_general(lhs, _block_diag(it["k"].astype(BF16), HEAD_DIM),
                                   (((1,), (1,)), ((), ())), preferred_element_type=F32)

    for it in items:
        cs, c0 = it["cs"], it["c0"]
        incl = (r >= c) if it["lower"] else (r <= c)
        strict = (r > c) if it["lower"] else (r < c)
        gcol = jnp.where(first, cs[:, CS_G + c0:CS_G + c0 + 1], cs[:, CS_G + c0 + 1:CS_G + c0 + 2])
        decay = jnp.where(incl, jnp.exp(jnp.where(incl, gcol - it["grow"], 0.0)), 0.0)
        a = jnp.where(strict, it["kq"][:CHUNK] * decay, 0.0)
        it["intra"] = (it["kq"][CHUNK:] * decay).astype(BF16)
        it["p"] = eye - a
        it["a"] = a.astype(BF16)

    for it in items:
        it["y"] = _dot(it["a"], _block_diag(it["a"], CHUNK)).astype(BF16)
    steps = 2
    while 2 * steps < CHUNK:
        steps *= 2
        for it in items:
            res = _dot(jnp.concatenate([it["p"].astype(BF16), it["y"]], axis=0), _block_diag(it["y"], CHUNK))
            it["p"] = it["p"] + res[:CHUNK]
            it["y"] = res[CHUNK:].astype(BF16)
    for it in items:
        it["p"] = it["p"] + _dot(it["p"].astype(BF16), _block_diag(it["y"], CHUNK))

    for it in items:
        vb = (it["v"] * it["beta"]).astype(BF16)
        kbg = (it["kb"] * it["eg"]).astype(BF16)
        rhs = jnp.concatenate([_block_diag(vb, HEAD_DIM), _block_diag(kbg, HEAD_DIM)], axis=1)
        uw = _dot(it["p"].astype(BF16), rhs)
        it["u"] = uw[:, :PAIR]
        it["w"] = uw[:, PAIR:]
        it["qg"] = it["q"] * it["eg"]
        kg = (it["k"] * col256(it["cs"], CS_EKG, it["c0"])).astype(BF16)
        it["kgs"] = jnp.concatenate([kg[:, :HEAD_DIM], kg[:, HEAD_DIM:]], axis=0)


def _delta_scan(items, s_ref):
    for it in items:
        it["state"] = s_ref[it["sidx"]]
        sb = it["state"].astype(BF16)
        ws = []
        for h in range(2):
            sl = slice(h * HEAD_DIM, (h + 1) * HEAD_DIM)
            lhs = jnp.concatenate([it["w"][:, sl], it["qg"][:, sl]], axis=0).astype(BF16)
            ws.append(_dot(lhs, sb[:, sl]))
        it["ws_top"] = jnp.concatenate([ws[0][:CHUNK], ws[1][:CHUNK]], axis=1)
        it["ws_bot"] = jnp.concatenate([ws[0][CHUNK:], ws[1][CHUNK:]], axis=1)

    outs = []
    for it in items:
        cs, c0 = it["cs"], it["c0"]
        v_new = _block_diag((it["u"] - it["ws_top"]).astype(BF16), HEAD_DIM)
        outs.append(it["ws_bot"] + _dot(it["intra"], v_new))
        upd = lax.dot_general(it["kgs"], v_new, (((0,), (0,)), ((), ())), preferred_element_type=F32)
        egt = jnp.concatenate([jnp.broadcast_to(cs[0:1, CS_EGT + c0:CS_EGT + c0 + 1], (1, HEAD_DIM)),
                               jnp.broadcast_to(cs[0:1, CS_EGT + c0 + 1:CS_EGT + c0 + 2], (1, HEAD_DIM))], axis=1)
        s_ref[it["sidx"]] = it["state"] * egt + upd
    return outs


def _delta_kernel(qf_ref, kf_ref, vf_ref, csf_ref, gtf_ref, qb_ref, kb_ref, vb_ref, csb_ref, gtb_ref,
                  of_ref, ob_ref, s_ref):
    @pl.when(pl.program_id(0) == 0)
    def _():
        s_ref[...] = jnp.zeros_like(s_ref)

    nbatch = qf_ref.shape[0]
    nc = TC_DELTA // CHUNK
    npairs = N_HEADS // 2

    def body(i, carry):
        rounds = []
        for cc in range(DELTA_CHUNKS_PER_ITER):
            jf = DELTA_CHUNKS_PER_ITER * i + cc
            jb = nc - 1 - jf
            rf = pl.multiple_of(jf * CHUNK, CHUNK)
            rb = pl.multiple_of(jb * CHUNK, CHUNK)
            items = []
            for bi in range(nbatch):
                for lower, j, rows, (q_ref, k_ref, v_ref, cs_ref, gt_ref, o_ref) in (
                        (True, jf, rf, (qf_ref, kf_ref, vf_ref, csf_ref, gtf_ref, of_ref)),
                        (False, jb, rb, (qb_ref, kb_ref, vb_ref, csb_ref, gtb_ref, ob_ref))):
                    d = 0 if lower else 1
                    cs = cs_ref[bi, pl.ds(rows, CHUNK), :]
                    gt = gt_ref[bi, j]
                    for hp in range(npairs):
                        sl = slice(hp * PAIR, (hp + 1) * PAIR)
                        items.append(dict(
                            q=q_ref[bi, pl.ds(rows, CHUNK), sl], k=k_ref[bi, pl.ds(rows, CHUNK), sl],
                            v=v_ref[bi, pl.ds(rows, CHUNK), sl], cs=cs,
                            grow=gt[d * npairs + hp:d * npairs + hp + 1, :], c0=d * N_HEADS + 2 * hp,
                            sidx=(bi * 2 + d) * npairs + hp, lower=lower,
                            dst=(o_ref, bi, rows, sl)))
            rounds.append(items)
        _delta_prep([it for items in rounds for it in items])
        for items in rounds:
            outs = _delta_scan(items, s_ref)
            for it, o in zip(items, outs):
                o_ref, bi, rows, sl = it["dst"]
                o_ref[bi, pl.ds(rows, CHUNK), sl] = o
        return carry

    lax.fori_loop(0, nc // DELTA_CHUNKS_PER_ITER, body, 0)


def _delta(q, k, v, cs, gt):
    b, s, _ = q.shape
    tc = TC_DELTA
    nb = s // tc
    ncb = tc // CHUNK
    fwd = lambda w: pl.BlockSpec((b, tc, w), lambda j: (0, j, 0))
    bwd = lambda w: pl.BlockSpec((b, tc, w), lambda j: (0, nb - 1 - j, 0))
    gt_f = pl.BlockSpec((b, ncb, N_HEADS, 2 * CHUNK), lambda j: (0, j, 0, 0))
    gt_b = pl.BlockSpec((b, ncb, N_HEADS, 2 * CHUNK), lambda j: (0, nb - 1 - j, 0, 0))
    return pl.pallas_call(
        _delta_kernel,
        grid=(nb,),
        in_specs=[fwd(D_DN), fwd(D_DN), fwd(D_DN), fwd(LANES), gt_f,
                  bwd(D_DN), bwd(D_DN), bwd(D_DN), bwd(LANES), gt_b],
        out_specs=[fwd(D_DN), bwd(D_DN)],
        out_shape=[jax.ShapeDtypeStruct((b, s, D_DN), F32), jax.ShapeDtypeStruct((b, s, D_DN), F32)],
        scratch_shapes=[pltpu.VMEM((b * N_HEADS, HEAD_DIM, PAIR), F32)],
        compiler_params=pltpu.CompilerParams(dimension_semantics=("arbitrary",),
                                             vmem_limit_bytes=VMEM_LIMIT),
        name="delta",
    )(q, k, v, cs, gt, q, k, v, cs, gt)


def _out_kernel(x_ref, p_ref, of_ref, ob_ref, z_ref, ya_ref, gates_ref,
                gdn_ref, wa_ref, wdn_ref, wo_ref, gffn_ref, w1_ref, w2_ref,
                gple_ref, wpg_ref, wpp_ref, gfin_ref, out_ref):
    o = of_ref[...] + ob_ref[...]
    z = z_ref[...].astype(F32)
    gdn = gdn_ref[...]
    parts = []
    for h in range(N_HEADS):
        sl = slice(h * HEAD_DIM, (h + 1) * HEAD_DIM)
        parts.append(_rms(o[:, sl], gdn) * _silu(z[:, sl]))
    y_dn = jnp.concatenate(parts, axis=1).astype(BF16)
    gates = gates_ref[...].astype(F32)
    merged = (jax.nn.sigmoid(gates[:, :D_MODEL]) * _dot(ya_ref[...], wa_ref[...])
              + jax.nn.sigmoid(gates[:, D_MODEL:]) * _dot(y_dn, wdn_ref[...]))
    h1 = x_ref[...] + _dot(merged.astype(BF16), wo_ref[...])

    n2 = _rms(h1, gffn_ref[...]).astype(BF16)
    h2 = h1
    for c in range(D_FF // FF_CHUNK):
        sl = slice(c * FF_CHUNK, (c + 1) * FF_CHUNK)
        a = jnp.maximum(_dot(n2, w1_ref[:, sl]), 0.0)
        h2 = h2 + _dot((a * a).astype(BF16), w2_ref[sl, :])

    n3 = _rms(h2, gple_ref[...]).astype(BF16)
    gate_p = jax.nn.sigmoid(_dot(n3, wpg_ref[...]))
    h3 = h2 + gate_p * _dot(p_ref[...].astype(BF16), wpp_ref[...])
    out_ref[...] = _rms(h3, gfin_ref[...])


def _out(x2, p2, o_f, o_b, z, ya, gates, gdn, wa, wdn, wo, gffn, w1, w2, gple, wpg, wpp, gfin):
    t = x2.shape[0]
    tm = TM_OUT
    row = lambda w: pl.BlockSpec((tm, w), lambda i: (i, 0))
    weights = (gdn, wa, wdn, wo, gffn, w1, w2, gple, wpg, wpp, gfin)
    return pl.pallas_call(
        _out_kernel,
        grid=(t // tm,),
        in_specs=[row(D_MODEL), row(PLE_DIM), row(D_DN), row(D_DN), row(D_DN), row(D_CONV),
                  row(2 * D_MODEL)] + [_resident(w.shape) for w in weights],
        out_specs=row(D_MODEL),
        out_shape=jax.ShapeDtypeStruct((t, D_MODEL), F32),
        compiler_params=pltpu.CompilerParams(dimension_semantics=("arbitrary",),
                                             vmem_limit_bytes=VMEM_LIMIT),
        name="out",
    )(x2, p2, o_f, o_b, z, ya, gates, *weights)


def kernel(x, p, g_mix, w_in, conv_a, conv_qkv, a_log_f, a_log_b, dt_bias_f, dt_bias_b, g_dn_norm,
           w_a_out, w_dn_out, w_o, g_ffn, w_ff1, w_ff2, g_ple, w_ple_gate, w_ple_proj, g_final):
    b, s, d = x.shape
    depth = p.shape[0]
    assert depth == 1 and d == D_MODEL and s % TC_DELTA == 0 and s % TS_FRONT == 0
    t = b * s
    i = 0

    w = w_in[i]
    c0 = 3 * D_CONV
    c1 = c0 + 3 * D_DN
    c2 = c1 + D_DN
    c3 = c2 + 4 * N_HEADS
    wpa = w[:, :c0].astype(BF16)
    wqkv = w[:, c0:c1].astype(BF16)
    wz = w[:, c1:c2].astype(BF16)
    wsm = jnp.pad(w[:, c2:c3], ((0, 0), (0, LANES - 4 * N_HEADS))).astype(BF16)
    wg = w[:, c3:].astype(BF16)

    zeros8 = jnp.zeros((2 * N_HEADS,), F32)
    tail = jnp.zeros((LANES - 4 * N_HEADS,), F32)
    alog = jnp.concatenate([zeros8, a_log_f[i], a_log_b[i], tail])[None, :]
    dtb = jnp.concatenate([zeros8, dt_bias_f[i], dt_bias_b[i], tail])[None, :]
    ya, q, k, v, cs, gt, z, gates = _front(x, g_mix[i][None, :], wpa, wqkv, wz, wsm, wg,
                                           conv_a[i], conv_qkv[i], alog, dtb)

    gt = gt.reshape(b, s // CHUNK, N_HEADS, 2 * CHUNK)
    o_f, o_b = _delta(q, k, v, cs, gt)

    out = _out(x.reshape(t, d), p[i].reshape(t, PLE_DIM), o_f.reshape(t, D_DN), o_b.reshape(t, D_DN),
               z.reshape(t, D_DN), ya.reshape(t, D_CONV), gates.reshape(t, 2 * D_MODEL),
               g_dn_norm[i][None, :], w_a_out[i].astype(BF16), w_dn_out[i].astype(BF16),
               w_o[i].astype(BF16), g_ffn[i][None, :], w_ff1[i].astype(BF16), w_ff2[i].astype(BF16),
               g_ple[i][None, :], w_ple_gate[i].astype(BF16), w_ple_proj[i].astype(BF16),
               g_final[None, :])
    return out.reshape(b, s, d)
```

```python
import functools

import jax
import jax.numpy as jnp
from jax import lax
from jax.experimental import pallas as pl
from jax.experimental.pallas import tpu as pltpu

F32 = jnp.float32
BF16 = jnp.bfloat16

D_MODEL = 1024
PLE_DIM = 256
D_CONV = 512
N_HEADS = 4
HEAD_DIM = 128
D_DN = N_HEADS * HEAD_DIM
CHUNK = 64
D_FF = 4 * D_MODEL
EPS = 1e-6

LANES = 128
VMEM_LIMIT = 56 * 1024 * 1024

TS_FRONT = 512
TC_DELTA = 256
DELTA_CHUNKS_PER_ITER = 2
TM_OUT = 512
FF_CHUNK = 1024

W_PA = 0
W_QKV = W_PA + 3 * D_CONV
W_Z = W_QKV + 3 * D_DN
W_SM = W_Z + D_DN
W_GATES = W_SM + 4 * N_HEADS

CS_BETA, CS_G, CS_EG, CS_EKG, CS_EGT = 0, 8, 16, 24, 32


def _resident(shape):
    nd = len(shape)
    return pl.BlockSpec(shape, lambda *_: (0,) * nd, pipeline_mode=pl.Buffered(1))


def _rms(x, g):
    return x * lax.rsqrt(jnp.mean(x * x, axis=-1, keepdims=True) + EPS) * g


def _dot(a, b):
    return jnp.dot(a, b, preferred_element_type=F32)


def _silu(x):
    return x * jax.nn.sigmoid(x)


HALO = 8


def _conv3(ext, w, ts):
    n = ext.shape[0]
    up = pltpu.roll(ext, 1, 0)[HALO:HALO + ts]
    dn = pltpu.roll(ext, n - 1, 0)[HALO:HALO + ts]
    return up * w[0:1, :] + ext[HALO:HALO + ts] * w[1:2, :] + dn * w[2:3, :]


def _chunk_prefix_sums(g):
    ts = g.shape[0]
    hi = g.astype(BF16).astype(F32)
    r1 = g - hi
    mid = r1.astype(BF16).astype(F32)
    lo = (r1 - mid).astype(BF16).astype(F32)
    packed = (hi + pltpu.roll(mid, 16, 1) + pltpu.roll(lo, 32, 1)).astype(BF16)
    r = lax.broadcasted_iota(jnp.int32, (CHUNK, CHUNK), 0)
    c = lax.broadcasted_iota(jnp.int32, (CHUNK, CHUNK), 1)
    tri = jnp.where(c <= r, 1.0, 0.0).astype(BF16)
    res = jnp.concatenate([_dot(tri, packed[n * CHUNK:(n + 1) * CHUNK]) for n in range(ts // CHUNK)], axis=0)
    return res + pltpu.roll(res, LANES - 16, 1) + pltpu.roll(res, LANES - 32, 1)


def _front_kernel(x_ref, xp_ref, xn_ref, gmix_ref, w_ref, wg_ref,
                  conv_a_ref, conv_qkv_ref, alog_ref, dtb_ref,
                  ya_ref, q_ref, k_ref, v_ref, cs_ref, gt_ref, z_ref, gates_ref):
    j = pl.program_id(1)
    nj = pl.num_programs(1)
    ts = TS_FRONT
    has_prev = (j > 0).astype(F32)
    has_next = (j < nj - 1).astype(F32)
    inner = slice(HALO, HALO + ts)

    gm = gmix_ref[...]
    n_cur = _rms(x_ref[0], gm)
    n_ext = jnp.concatenate([_rms(xp_ref[0], gm) * has_prev, n_cur, _rms(xn_ref[0], gm) * has_next],
                            axis=0).astype(BF16)
    n_b = n_cur.astype(BF16)

    sm = _dot(n_b, w_ref[:, W_SM:W_SM + LANES])
    lane = lax.broadcasted_iota(jnp.int32, sm.shape, 1)
    beta = jax.nn.sigmoid(sm)
    xs = sm + dtb_ref[...]
    softplus = jnp.maximum(xs, 0.0) + jnp.log1p(jnp.exp(-jnp.abs(xs)))
    g = jnp.where((lane >= 8) & (lane < 16), -jnp.exp(alog_ref[...]) * softplus, 0.0)

    pa = _dot(n_ext, w_ref[:, W_PA:W_QKV])
    u = pa[:, :D_CONV] * pa[:, 2 * D_CONV:]
    ya = pa[inner, D_CONV:2 * D_CONV] * _conv3(u, conv_a_ref[...], ts)
    ya_ref[0] = ya.astype(ya_ref.dtype)

    pre = _chunk_prefix_sums(g)
    pre3 = pre.reshape(ts // CHUNK, CHUNK, LANES)
    tot = jnp.broadcast_to(pre3[:, CHUNK - 1:CHUNK, :], pre3.shape).reshape(ts, LANES)
    gc = jnp.where(lane < 12, pre, tot - pre + g)
    eg = jnp.exp(gc)
    ekg = jnp.exp(tot - gc)
    egt = jnp.exp(tot)

    g8 = pltpu.roll(gc, LANES - 8, 1)
    cs = jnp.where(lane < 8, beta, 0.0)
    cs = jnp.where((lane >= 8) & (lane < 16), gc, cs)
    cs = jnp.where((lane >= 16) & (lane < 24), pltpu.roll(eg, 8, 1), cs)
    cs = jnp.where((lane >= 24) & (lane < 32), pltpu.roll(ekg, 16, 1), cs)
    cs = jnp.where((lane >= 32) & (lane < 40), pltpu.roll(egt, 24, 1), cs)
    cs_ref[0] = cs

    g8t = g8.T
    for n in range(ts // CHUNK):
        gt_ref[0, n] = g8t[0:8, n * CHUNK:(n + 1) * CHUNK]

    for gi, (ref, scale) in enumerate(((q_ref, HEAD_DIM ** -0.5), (k_ref, None), (v_ref, None))):
        sl = slice(gi * D_DN, (gi + 1) * D_DN)
        c = _silu(_conv3(_dot(n_ext, w_ref[:, W_QKV + gi * D_DN:W_QKV + (gi + 1) * D_DN]), conv_qkv_ref[:, sl], ts))
        if ref is v_ref:
            v_ref[0] = c
            continue
        for h in range(N_HEADS):
            hs = slice(h * HEAD_DIM, (h + 1) * HEAD_DIM)
            ch = c[:, hs]
            cn = ch * lax.rsqrt(jnp.sum(ch * ch, axis=-1, keepdims=True) + EPS)
            ref[0, :, hs] = cn if scale is None else cn * scale

    z_ref[0] = _dot(n_b, w_ref[:, W_Z:W_SM]).astype(z_ref.dtype)
    gates_ref[0] = _dot(n_b, wg_ref[...]).astype(gates_ref.dtype)


def _front(x, g_mix, w, wg, conv_a, conv_qkv, alog, dtb):
    b, s, d = x.shape
    ts = TS_FRONT
    nj = s // ts
    r8 = ts // HALO
    n8 = s // HALO

    def cur(w):
        return pl.BlockSpec((1, ts, w), lambda i, j: (i, j, 0))

    prev = pl.BlockSpec((1, HALO, d), lambda i, j: (i, jnp.maximum(j * r8 - 1, 0), 0))
    nxt = pl.BlockSpec((1, HALO, d), lambda i, j: (i, jnp.minimum((j + 1) * r8, n8 - 1), 0))
    consts = (g_mix, w, wg, conv_a, conv_qkv, alog, dtb)
    return pl.pallas_call(
        _front_kernel,
        grid=(b, nj),
        in_specs=[cur(d), prev, nxt] + [_resident(c.shape) for c in consts],
        out_specs=[cur(D_CONV), cur(D_DN), cur(D_DN), cur(D_DN), cur(LANES),
                   pl.BlockSpec((1, ts // CHUNK, 8, CHUNK), lambda i, j: (i, j, 0, 0)),
                   cur(D_DN), cur(2 * D_MODEL)],
        out_shape=[jax.ShapeDtypeStruct((b, s, D_CONV), BF16),
                   jax.ShapeDtypeStruct((b, s, D_DN), F32),
                   jax.ShapeDtypeStruct((b, s, D_DN), F32),
                   jax.ShapeDtypeStruct((b, s, D_DN), F32),
                   jax.ShapeDtypeStruct((b, s, LANES), F32),
                   jax.ShapeDtypeStruct((b, s // CHUNK, 8, CHUNK), F32),
                   jax.ShapeDtypeStruct((b, s, D_DN), BF16),
                   jax.ShapeDtypeStruct((b, s, 2 * D_MODEL), BF16)],
        compiler_params=pltpu.CompilerParams(dimension_semantics=("arbitrary", "arbitrary"),
                                             vmem_limit_bytes=VMEM_LIMIT),
        name="front",
    )(x, x, x, *consts)


PAIR = 2 * HEAD_DIM


def _block_diag(x, half):
    lane = lax.broadcasted_iota(jnp.int32, x.shape, 1)
    zero = jnp.zeros_like(x)
    return jnp.concatenate([jnp.where(lane < half, x, zero), jnp.where(lane >= half, x, zero)], axis=0)


def _delta_prep(items):
    r = lax.broadcasted_iota(jnp.int32, (CHUNK, 2 * CHUNK), 0)
    lane = lax.broadcasted_iota(jnp.int32, (CHUNK, 2 * CHUNK), 1)
    c = jnp.where(lane < CHUNK, lane, lane - CHUNK)
    first = lane < CHUNK
    eye = jnp.where(r == c, 1.0, 0.0)

    def col256(cs, base, c0):
        return jnp.concatenate([jnp.broadcast_to(cs[:, base + c0:base + c0 + 1], (CHUNK, HEAD_DIM)),
                                jnp.broadcast_to(cs[:, base + c0 + 1:base + c0 + 2], (CHUNK, HEAD_DIM))], axis=1)

    for it in items:
        cs, c0 = it["cs"], it["c0"]
        it["beta"] = col256(cs, CS_BETA, c0)
        it["eg"] = col256(cs, CS_EG, c0)
        it["kb"] = it["k"] * it["beta"]
        lhs = jnp.concatenate([it["kb"], it["q"]], axis=0).astype(BF16)
        it["kq"] = lax.dot_general(lhs, _block_diag(it["k"].astype(BF16), HEAD_DIM),
                                   (((1,), (1,)), ((), ())), preferred_element_type=F32)

    for it in items:
        cs, c0 = it["cs"], it["c0"]
        incl = (r >= c) if it["lower"] else (r <= c)
        strict = (r > c) if it["lower"] else (r < c)
        gcol = jnp.where(first, cs[:, CS_G + c0:CS_G + c0 + 1], cs[:, CS_G + c0 + 1:CS_G + c0 + 2])
        decay = jnp.where(incl, jnp.exp(jnp.where(incl, gcol - it["grow"], 0.0)), 0.0)
        a = jnp.where(strict, it["kq"][:CHUNK] * decay, 0.0)
        it["intra"] = (it["kq"][CHUNK:] * decay).astype(BF16)
        it["p"] = eye - a
        it["a"] = a.astype(BF16)

    for it in items:
        it["y"] = _dot(it["a"], _block_diag(it["a"], CHUNK)).astype(BF16)
    steps = 2
    while 2 * steps < CHUNK:
        steps *= 2
        for it in items:
            res = _dot(jnp.concatenate([it["p"].astype(BF16), it["y"]], axis=0), _block_diag(it["y"], CHUNK))
            it["p"] = it["p"] + res[:CHUNK]
            it["y"] = res[CHUNK:].astype(BF16)
    for it in items:
        it["p"] = it["p"] + _dot(it["p"].astype(BF16), _block_diag(it["y"], CHUNK))

    for it in items:
        vb = (it["v"] * it["beta"]).astype(BF16)
        kbg = (it["kb"] * it["eg"]).astype(BF16)
        rhs = jnp.concatenate([_block_diag(vb, HEAD_DIM), _block_diag(kbg, HEAD_DIM)], axis=1)
        uw = _dot(it["p"].astype(BF16), rhs)
        it["u"] = uw[:, :PAIR]
        it["w"] = uw[:, PAIR:]
        it["qg"] = it["q"] * it["eg"]
        kg = (it["k"] * col256(it["cs"], CS_EKG, it["c0"])).astype(BF16)
        it["kgs"] = jnp.concatenate([kg[:, :HEAD_DIM], kg[:, HEAD_DIM:]], axis=0)


def _delta_scan(items, s_ref):
    for it in items:
        it["state"] = s_ref[it["sidx"]]
        sb = it["state"].astype(BF16)
        ws = []
        for h in range(2):
            sl = slice(h * HEAD_DIM, (h + 1) * HEAD_DIM)
            lhs = jnp.concatenate([it["w"][:, sl], it["qg"][:, sl]], axis=0).astype(BF16)
            ws.append(_dot(lhs, sb[:, sl]))
        it["ws_top"] = jnp.concatenate([ws[0][:CHUNK], ws[1][:CHUNK]], axis=1)
        it["ws_bot"] = jnp.concatenate([ws[0][CHUNK:], ws[1][CHUNK:]], axis=1)

    outs = []
    for it in items:
        cs, c0 = it["cs"], it["c0"]
        v_new = _block_diag((it["u"] - it["ws_top"]).astype(BF16), HEAD_DIM)
        outs.append(it["ws_bot"] + _dot(it["intra"], v_new))
        upd = lax.dot_general(it["kgs"], v_new, (((0,), (0,)), ((), ())), preferred_element_type=F32)
        egt = jnp.concatenate([jnp.broadcast_to(cs[0:1, CS_EGT + c0:CS_EGT + c0 + 1], (1, HEAD_DIM)),
                               jnp.broadcast_to(cs[0:1, CS_EGT + c0 + 1:CS_EGT + c0 + 2], (1, HEAD_DIM))], axis=1)
        s_ref[it["sidx"]] = it["state"] * egt + upd
    return outs


def _delta_kernel(qf_ref, kf_ref, vf_ref, csf_ref, gtf_ref, qb_ref, kb_ref, vb_ref, csb_ref, gtb_ref,
                  of_ref, ob_ref, s_ref):
    @pl.when(pl.program_id(0) == 0)
    def _():
        s_ref[...] = jnp.zeros_like(s_ref)

    nbatch = qf_ref.shape[0]
    nc = TC_DELTA // CHUNK
    npairs = N_HEADS // 2

    def body(i, carry):
        rounds = []
        for cc in range(DELTA_CHUNKS_PER_ITER):
            jf = DELTA_CHUNKS_PER_ITER * i + cc
            jb = nc - 1 - jf
            rf = pl.multiple_of(jf * CHUNK, CHUNK)
            rb = pl.multiple_of(jb * CHUNK, CHUNK)
            items = []
            for bi in range(nbatch):
                for lower, j, rows, (q_ref, k_ref, v_ref, cs_ref, gt_ref, o_ref) in (
                        (True, jf, rf, (qf_ref, kf_ref, vf_ref, csf_ref, gtf_ref, of_ref)),
                        (False, jb, rb, (qb_ref, kb_ref, vb_ref, csb_ref, gtb_ref, ob_ref))):
                    d = 0 if lower else 1
                    cs = cs_ref[bi, pl.ds(rows, CHUNK), :]
                    gt = gt_ref[bi, j]
                    for hp in range(npairs):
                        sl = slice(hp * PAIR, (hp + 1) * PAIR)
                        items.append(dict(
                            q=q_ref[bi, pl.ds(rows, CHUNK), sl], k=k_ref[bi, pl.ds(rows, CHUNK), sl],
                            v=v_ref[bi, pl.ds(rows, CHUNK), sl], cs=cs,
                            grow=gt[d * npairs + hp:d * npairs + hp + 1, :], c0=d * N_HEADS + 2 * hp,
                            sidx=(bi * 2 + d) * npairs + hp, lower=lower,
                            dst=(o_ref, bi, rows, sl)))
            rounds.append(items)
        _delta_prep([it for items in rounds for it in items])
        for items in rounds:
            outs = _delta_scan(items, s_ref)
            for it, o in zip(items, outs):
                o_ref, bi, rows, sl = it["dst"]
                o_ref[bi, pl.ds(rows, CHUNK), sl] = o
        return carry

    lax.fori_loop(0, nc // DELTA_CHUNKS_PER_ITER, body, 0)


def _delta(q, k, v, cs, gt):
    b, s, _ = q.shape
    tc = TC_DELTA
    nb = s // tc
    ncb = tc // CHUNK
    fwd = lambda w: pl.BlockSpec((b, tc, w), lambda j: (0, j, 0))
    bwd = lambda w: pl.BlockSpec((b, tc, w), lambda j: (0, nb - 1 - j, 0))
    gt_f = pl.BlockSpec((b, ncb, N_HEADS, 2 * CHUNK), lambda j: (0, j, 0, 0))
    gt_b = pl.BlockSpec((b, ncb, N_HEADS, 2 * CHUNK), lambda j: (0, nb - 1 - j, 0, 0))
    return pl.pallas_call(
        _delta_kernel,
        grid=(nb,),
        in_specs=[fwd(D_DN), fwd(D_DN), fwd(D_DN), fwd(LANES), gt_f,
                  bwd(D_DN), bwd(D_DN), bwd(D_DN), bwd(LANES), gt_b],
        out_specs=[fwd(D_DN), bwd(D_DN)],
        out_shape=[jax.ShapeDtypeStruct((b, s, D_DN), F32), jax.ShapeDtypeStruct((b, s, D_DN), F32)],
        scratch_shapes=[pltpu.VMEM((b * N_HEADS, HEAD_DIM, PAIR), F32)],
        compiler_params=pltpu.CompilerParams(dimension_semantics=("arbitrary",),
                                             vmem_limit_bytes=VMEM_LIMIT),
        name="delta",
    )(q, k, v, cs, gt, q, k, v, cs, gt)


def _out_kernel(x_ref, p_ref, of_ref, ob_ref, z_ref, ya_ref, gates_ref,
                gdn_ref, wa_ref, wdn_ref, wo_ref, gffn_ref, w1_ref, w2_ref,
                gple_ref, wpg_ref, wpp_ref, gfin_ref, out_ref):
    o = of_ref[...] + ob_ref[...]
    z = z_ref[...].astype(F32)
    gdn = gdn_ref[...]
    parts = []
    for h in range(N_HEADS):
        sl = slice(h * HEAD_DIM, (h + 1) * HEAD_DIM)
        parts.append(_rms(o[:, sl], gdn) * _silu(z[:, sl]))
    y_dn = jnp.concatenate(parts, axis=1).astype(BF16)
    gates = gates_ref[...].astype(F32)
    merged = (jax.nn.sigmoid(gates[:, :D_MODEL]) * _dot(ya_ref[...], wa_ref[...])
              + jax.nn.sigmoid(gates[:, D_MODEL:]) * _dot(y_dn, wdn_ref[...]))
    h1 = x_ref[...] + _dot(merged.astype(BF16), wo_ref[...])

    n2 = _rms(h1, gffn_ref[...]).astype(BF16)
    h2 = h1
    for c in range(D_FF // FF_CHUNK):
        sl = slice(c * FF_CHUNK, (c + 1) * FF_CHUNK)
        a = jnp.maximum(_dot(n2, w1_ref[:, sl]), 0.0)
        h2 = h2 + _dot((a * a).astype(BF16), w2_ref[sl, :])

    n3 = _rms(h2, gple_ref[...]).astype(BF16)
    gate_p = jax.nn.sigmoid(_dot(n3, wpg_ref[...]))
    h3 = h2 + gate_p * _dot(p_ref[...].astype(BF16), wpp_ref[...])
    out_ref[...] = _rms(h3, gfin_ref[...])


def _out(x2, p2, o_f, o_b, z, ya, gates, gdn, wa, wdn, wo, gffn, w1, w2, gple, wpg, wpp, gfin):
    t = x2.shape[0]
    tm = TM_OUT
    row = lambda w: pl.BlockSpec((tm, w), lambda i: (i, 0))
    weights = (gdn, wa, wdn, wo, gffn, w1, w2, gple, wpg, wpp, gfin)
    return pl.pallas_call(
        _out_kernel,
        grid=(t // tm,),
        in_specs=[row(D_MODEL), row(PLE_DIM), row(D_DN), row(D_DN), row(D_DN), row(D_CONV),
                  row(2 * D_MODEL)] + [_resident(w.shape) for w in weights],
        out_specs=row(D_MODEL),
        out_shape=jax.ShapeDtypeStruct((t, D_MODEL), F32),
        compiler_params=pltpu.CompilerParams(dimension_semantics=("arbitrary",),
                                             vmem_limit_bytes=VMEM_LIMIT),
        name="out",
    )(x2, p2, o_f, o_b, z, ya, gates, *weights)


def kernel(x, p, g_mix, w_in, conv_a, conv_qkv, a_log_f, a_log_b, dt_bias_f, dt_bias_b, g_dn_norm,
           w_a_out, w_dn_out, w_o, g_ffn, w_ff1, w_ff2, g_ple, w_ple_gate, w_ple_proj, g_final):
    b, s, d = x.shape
    depth = p.shape[0]
    assert depth == 1 and d == D_MODEL and s % TC_DELTA == 0 and s % TS_FRONT == 0
    t = b * s
    i = 0

    w = w_in[i].astype(BF16)
    wg = w[:, W_GATES:]

    zeros8 = jnp.zeros((2 * N_HEADS,), F32)
    tail = jnp.zeros((LANES - 4 * N_HEADS,), F32)
    alog = jnp.concatenate([zeros8, a_log_f[i], a_log_b[i], tail])[None, :]
    dtb = jnp.concatenate([zeros8, dt_bias_f[i], dt_bias_b[i], tail])[None, :]
    ya, q, k, v, cs, gt, z, gates = _front(x, g_mix[i][None, :], w, wg,
                                           conv_a[i], conv_qkv[i], alog, dtb)

    gt = gt.reshape(b, s // CHUNK, N_HEADS, 2 * CHUNK)
    o_f, o_b = _delta(q, k, v, cs, gt)

    out = _out(x.reshape(t, d), p[i].reshape(t, PLE_DIM), o_f.reshape(t, D_DN), o_b.reshape(t, D_DN),
               z.reshape(t, D_DN), ya.reshape(t, D_CONV), gates.reshape(t, 2 * D_MODEL),
               g_dn_norm[i][None, :], w_a_out[i].astype(BF16), w_dn_out[i].astype(BF16),
               w_o[i].astype(BF16), g_ffn[i][None, :], w_ff1[i].astype(BF16), w_ff2[i].astype(BF16),
               g_ple[i][None, :], w_ple_gate[i].astype(BF16), w_ple_proj[i].astype(BF16),
               g_final[None, :])
    return out.reshape(b, s, d)
```

```python
import functools

import jax
import jax.numpy as jnp
from jax import lax
from jax.experimental import pallas as pl
from jax.experimental.pallas import tpu as pltpu

F32 = jnp.float32
BF16 = jnp.bfloat16

D_MODEL = 1024
PLE_DIM = 256
D_CONV = 512
N_HEADS = 4
HEAD_DIM = 128
D_DN = N_HEADS * HEAD_DIM
CHUNK = 64
D_FF = 4 * D_MODEL
EPS = 1e-6

LANES = 128
VMEM_LIMIT = 56 * 1024 * 1024

TS_FRONT = 512
TC_DELTA = 256
DELTA_CHUNKS_PER_ITER = 2
TM_OUT = 512
FF_CHUNK = 1024

W_PA = 0
W_QKV = W_PA + 3 * D_CONV
W_Z = W_QKV + 3 * D_DN
W_SM = W_Z + D_DN
W_GATES = W_SM + 4 * N_HEADS

CS_BETA, CS_G, CS_EG, CS_EKG, CS_EGT = 0, 8, 16, 24, 32


def _resident(shape):
    nd = len(shape)
    return pl.BlockSpec(shape, lambda *_: (0,) * nd, pipeline_mode=pl.Buffered(1))


def _rms(x, g):
    return x * lax.rsqrt(jnp.mean(x * x, axis=-1, keepdims=True) + EPS) * g


def _dot(a, b):
    return jnp.dot(a, b, preferred_element_type=F32)


def _silu(x):
    return x * jax.nn.sigmoid(x)


HALO = 8


def _conv3(ext, w, ts):
    n = ext.shape[0]
    up = pltpu.roll(ext, 1, 0)[HALO:HALO + ts]
    dn = pltpu.roll(ext, n - 1, 0)[HALO:HALO + ts]
    return up * w[0:1, :] + ext[HALO:HALO + ts] * w[1:2, :] + dn * w[2:3, :]


def _chunk_prefix_sums(g):
    ts = g.shape[0]
    hi = g.astype(BF16).astype(F32)
    r1 = g - hi
    mid = r1.astype(BF16).astype(F32)
    lo = (r1 - mid).astype(BF16).astype(F32)
    packed = (hi + pltpu.roll(mid, 16, 1) + pltpu.roll(lo, 32, 1)).astype(BF16)
    r = lax.broadcasted_iota(jnp.int32, (CHUNK, CHUNK), 0)
    c = lax.broadcasted_iota(jnp.int32, (CHUNK, CHUNK), 1)
    tri = jnp.where(c <= r, 1.0, 0.0).astype(BF16)
    res = jnp.concatenate([_dot(tri, packed[n * CHUNK:(n + 1) * CHUNK]) for n in range(ts // CHUNK)], axis=0)
    return res + pltpu.roll(res, LANES - 16, 1) + pltpu.roll(res, LANES - 32, 1)


def _front_kernel(x_ref, xp_ref, xn_ref, gmix_ref, w_ref, wg_ref,
                  conv_a_ref, conv_qkv_ref, alog_ref, dtb_ref, *rest):
    nw = (len(rest) - 8) // 2
    wsrc_refs, (ya_ref, q_ref, k_ref, v_ref, cs_ref, gt_ref, z_ref, gates_ref), wdst_refs = (
        rest[:nw], rest[nw:nw + 8], rest[nw + 8:])
    j = pl.program_id(1)
    nj = pl.num_programs(1)
    ts = TS_FRONT
    has_prev = (j > 0).astype(F32)
    has_next = (j < nj - 1).astype(F32)
    inner = slice(HALO, HALO + ts)

    gm = gmix_ref[...]
    n_cur = _rms(x_ref[0], gm)
    n_ext = jnp.concatenate([_rms(xp_ref[0], gm) * has_prev, n_cur, _rms(xn_ref[0], gm) * has_next],
                            axis=0).astype(BF16)
    n_b = n_cur.astype(BF16)

    sm = _dot(n_b, w_ref[:, W_SM:W_SM + LANES])
    lane = lax.broadcasted_iota(jnp.int32, sm.shape, 1)
    beta = jax.nn.sigmoid(sm)
    xs = sm + dtb_ref[...]
    softplus = jnp.maximum(xs, 0.0) + jnp.log1p(jnp.exp(-jnp.abs(xs)))
    g = jnp.where((lane >= 8) & (lane < 16), -jnp.exp(alog_ref[...]) * softplus, 0.0)

    pa = _dot(n_ext, w_ref[:, W_PA:W_QKV])
    u = pa[:, :D_CONV] * pa[:, 2 * D_CONV:]
    ya = pa[inner, D_CONV:2 * D_CONV] * _conv3(u, conv_a_ref[...], ts)
    ya_ref[0] = ya.astype(ya_ref.dtype)

    pre = _chunk_prefix_sums(g)
    pre3 = pre.reshape(ts // CHUNK, CHUNK, LANES)
    tot = jnp.broadcast_to(pre3[:, CHUNK - 1:CHUNK, :], pre3.shape).reshape(ts, LANES)
    gc = jnp.where(lane < 12, pre, tot - pre + g)
    eg = jnp.exp(gc)
    ekg = jnp.exp(tot - gc)
    egt = jnp.exp(tot)

    g8 = pltpu.roll(gc, LANES - 8, 1)
    cs = jnp.where(lane < 8, beta, 0.0)
    cs = jnp.where((lane >= 8) & (lane < 16), gc, cs)
    cs = jnp.where((lane >= 16) & (lane < 24), pltpu.roll(eg, 8, 1), cs)
    cs = jnp.where((lane >= 24) & (lane < 32), pltpu.roll(ekg, 16, 1), cs)
    cs = jnp.where((lane >= 32) & (lane < 40), pltpu.roll(egt, 24, 1), cs)
    cs_ref[0] = cs

    g8t = g8.T
    for n in range(ts // CHUNK):
        gt_ref[0, n] = g8t[0:8, n * CHUNK:(n + 1) * CHUNK]

    for gi, (ref, scale) in enumerate(((q_ref, HEAD_DIM ** -0.5), (k_ref, None), (v_ref, None))):
        sl = slice(gi * D_DN, (gi + 1) * D_DN)
        c = _silu(_conv3(_dot(n_ext, w_ref[:, W_QKV + gi * D_DN:W_QKV + (gi + 1) * D_DN]), conv_qkv_ref[:, sl], ts))
        if ref is v_ref:
            v_ref[0] = c
            continue
        for h in range(N_HEADS):
            hs = slice(h * HEAD_DIM, (h + 1) * HEAD_DIM)
            ch = c[:, hs]
            cn = ch * lax.rsqrt(jnp.sum(ch * ch, axis=-1, keepdims=True) + EPS)
            ref[0, :, hs] = cn if scale is None else cn * scale

    z_ref[0] = _dot(n_b, w_ref[:, W_Z:W_SM]).astype(z_ref.dtype)
    gates_ref[0] = _dot(n_b, wg_ref[...]).astype(gates_ref.dtype)

    for src, dst in zip(wsrc_refs, wdst_refs):
        dst[...] = src[...].astype(dst.dtype)


def _front(x, g_mix, w, wg, conv_a, conv_qkv, alog, dtb, cast_weights):
    b, s, d = x.shape
    ts = TS_FRONT
    nj = s // ts
    r8 = ts // HALO
    n8 = s // HALO

    def cur(w):
        return pl.BlockSpec((1, ts, w), lambda i, j: (i, j, 0))

    prev = pl.BlockSpec((1, HALO, d), lambda i, j: (i, jnp.maximum(j * r8 - 1, 0), 0))
    nxt = pl.BlockSpec((1, HALO, d), lambda i, j: (i, jnp.minimum((j + 1) * r8, n8 - 1), 0))
    consts = (g_mix, w, wg, conv_a, conv_qkv, alog, dtb)
    slab = lambda a: pl.BlockSpec((a.shape[0] // (b * nj), a.shape[1]), lambda i, j: (i * nj + j, 0))
    return pl.pallas_call(
        _front_kernel,
        grid=(b, nj),
        in_specs=[cur(d), prev, nxt] + [_resident(c.shape) for c in consts] + [slab(a) for a in cast_weights],
        out_specs=[cur(D_CONV), cur(D_DN), cur(D_DN), cur(D_DN), cur(LANES),
                   pl.BlockSpec((1, ts // CHUNK, 8, CHUNK), lambda i, j: (i, j, 0, 0)),
                   cur(D_DN), cur(2 * D_MODEL)] + [slab(a) for a in cast_weights],
        out_shape=[jax.ShapeDtypeStruct((b, s, D_CONV), BF16),
                   jax.ShapeDtypeStruct((b, s, D_DN), F32),
                   jax.ShapeDtypeStruct((b, s, D_DN), F32),
                   jax.ShapeDtypeStruct((b, s, D_DN), F32),
                   jax.ShapeDtypeStruct((b, s, LANES), F32),
                   jax.ShapeDtypeStruct((b, s // CHUNK, 8, CHUNK), F32),
                   jax.ShapeDtypeStruct((b, s, D_DN), BF16),
                   jax.ShapeDtypeStruct((b, s, 2 * D_MODEL), BF16)]
        + [jax.ShapeDtypeStruct(a.shape, BF16) for a in cast_weights],
        compiler_params=pltpu.CompilerParams(dimension_semantics=("arbitrary", "arbitrary"),
                                             vmem_limit_bytes=VMEM_LIMIT),
        name="front",
    )(x, x, x, *consts, *cast_weights)


PAIR = 2 * HEAD_DIM


def _block_diag(x, half):
    lane = lax.broadcasted_iota(jnp.int32, x.shape, 1)
    zero = jnp.zeros_like(x)
    return jnp.concatenate([jnp.where(lane < half, x, zero), jnp.where(lane >= half, x, zero)], axis=0)


def _delta_prep(items):
    r = lax.broadcasted_iota(jnp.int32, (CHUNK, 2 * CHUNK), 0)
    lane = lax.broadcasted_iota(jnp.int32, (CHUNK, 2 * CHUNK), 1)
    c = jnp.where(lane < CHUNK, lane, lane - CHUNK)
    first = lane < CHUNK
    eye = jnp.where(r == c, 1.0, 0.0)

    def col256(cs, base, c0):
        return jnp.concatenate([jnp.broadcast_to(cs[:, base + c0:base + c0 + 1], (CHUNK, HEAD_DIM)),
                                jnp.broadcast_to(cs[:, base + c0 + 1:base + c0 + 2], (CHUNK, HEAD_DIM))], axis=1)

    def heads_over(x):
        return jnp.concatenate([x[:, :HEAD_DIM], x[:, HEAD_DIM:]], axis=0)

    for it in items:
        cs, c0 = it["cs"], it["c0"]
        it["beta"] = col256(cs, CS_BETA, c0)
        it["eg"] = col256(cs, CS_EG, c0)
        it["kb"] = it["k"] * it["beta"]
        kbq = jnp.concatenate([it["kb"], it["q"]], axis=0)
        it["kq"] = lax.dot_general(heads_over(kbq).astype(BF16), heads_over(it["k"]).astype(BF16),
                                   (((1,), (1,)), ((), ())), preferred_element_type=F32)

    for it in items:
        cs, c0 = it["cs"], it["c0"]
        incl = (r >= c) if it["lower"] else (r <= c)
        strict = (r > c) if it["lower"] else (r < c)
        gcol = jnp.where(first, cs[:, CS_G + c0:CS_G + c0 + 1], cs[:, CS_G + c0 + 1:CS_G + c0 + 2])
        decay = jnp.where(incl, jnp.exp(jnp.where(incl, gcol - it["grow"], 0.0)), 0.0)
        kq = it["kq"]
        kk = jnp.where(first, kq[:CHUNK], kq[2 * CHUNK:3 * CHUNK])
        qk = jnp.where(first, kq[CHUNK:2 * CHUNK], kq[3 * CHUNK:])
        a = jnp.where(strict, kk * decay, 0.0)
        it["intra"] = (qk * decay).astype(BF16)
        it["p"] = eye - a
        it["a"] = a.astype(BF16)

    for it in items:
        it["y"] = _dot(it["a"], _block_diag(it["a"], CHUNK)).astype(BF16)
    steps = 2
    while 2 * steps < CHUNK:
        steps *= 2
        for it in items:
            res = _dot(jnp.concatenate([it["p"].astype(BF16), it["y"]], axis=0), _block_diag(it["y"], CHUNK))
            it["p"] = it["p"] + res[:CHUNK]
            it["y"] = res[CHUNK:].astype(BF16)
    for it in items:
        it["p"] = it["p"] + _dot(it["p"].astype(BF16), _block_diag(it["y"], CHUNK))

    for it in items:
        vb = it["v"] * it["beta"]
        kbg = it["kb"] * it["eg"]
        rhs = jnp.concatenate([heads_over(vb), heads_over(kbg)], axis=1).astype(BF16)
        uw = _dot(_block_diag(it["p"].astype(BF16), CHUNK), rhs)
        it["u"] = jnp.concatenate([uw[:CHUNK, :HEAD_DIM], uw[CHUNK:, :HEAD_DIM]], axis=1)
        it["w"] = jnp.concatenate([uw[:CHUNK, HEAD_DIM:], uw[CHUNK:, HEAD_DIM:]], axis=1)
        it["qg"] = it["q"] * it["eg"]
        kg = (it["k"] * col256(it["cs"], CS_EKG, it["c0"])).astype(BF16)
        it["kgs_t"] = heads_over(kg).T


def _delta_scan(items, s_ref):
    for it in items:
        it["state"] = s_ref[it["sidx"]]
        sb = it["state"].astype(BF16)
        ws = []
        for h in range(2):
            sl = slice(h * HEAD_DIM, (h + 1) * HEAD_DIM)
            lhs = jnp.concatenate([it["w"][:, sl], it["qg"][:, sl]], axis=0).astype(BF16)
            ws.append(_dot(lhs, sb[:, sl]))
        it["ws_top"] = jnp.concatenate([ws[0][:CHUNK], ws[1][:CHUNK]], axis=1)
        it["ws_bot"] = jnp.concatenate([ws[0][CHUNK:], ws[1][CHUNK:]], axis=1)

    outs = []
    for it in items:
        cs, c0 = it["cs"], it["c0"]
        v_new = _block_diag((it["u"] - it["ws_top"]).astype(BF16), HEAD_DIM)
        ou = _dot(jnp.concatenate([it["intra"], it["kgs_t"]], axis=0), v_new)
        outs.append(it["ws_bot"] + ou[:CHUNK])
        upd = ou[CHUNK:]
        egt = jnp.concatenate([jnp.broadcast_to(cs[0:1, CS_EGT + c0:CS_EGT + c0 + 1], (1, HEAD_DIM)),
                               jnp.broadcast_to(cs[0:1, CS_EGT + c0 + 1:CS_EGT + c0 + 2], (1, HEAD_DIM))], axis=1)
        s_ref[it["sidx"]] = it["state"] * egt + upd
    return outs


def _delta_kernel(qf_ref, kf_ref, vf_ref, csf_ref, gtf_ref, qb_ref, kb_ref, vb_ref, csb_ref, gtb_ref,
                  of_ref, ob_ref, s_ref):
    @pl.when(pl.program_id(0) == 0)
    def _():
        s_ref[...] = jnp.zeros_like(s_ref)

    nbatch = qf_ref.shape[0]
    nc = TC_DELTA // CHUNK
    npairs = N_HEADS // 2

    def body(i, carry):
        rounds = []
        for cc in range(DELTA_CHUNKS_PER_ITER):
            jf = DELTA_CHUNKS_PER_ITER * i + cc
            jb = nc - 1 - jf
            rf = pl.multiple_of(jf * CHUNK, CHUNK)
            rb = pl.multiple_of(jb * CHUNK, CHUNK)
            items = []
            for bi in range(nbatch):
                for lower, j, rows, (q_ref, k_ref, v_ref, cs_ref, gt_ref, o_ref) in (
                        (True, jf, rf, (qf_ref, kf_ref, vf_ref, csf_ref, gtf_ref, of_ref)),
                        (False, jb, rb, (qb_ref, kb_ref, vb_ref, csb_ref, gtb_ref, ob_ref))):
                    d = 0 if lower else 1
                    cs = cs_ref[bi, pl.ds(rows, CHUNK), :]
                    gt = gt_ref[bi, j]
                    for hp in range(npairs):
                        sl = slice(hp * PAIR, (hp + 1) * PAIR)
                        items.append(dict(
                            q=q_ref[bi, pl.ds(rows, CHUNK), sl], k=k_ref[bi, pl.ds(rows, CHUNK), sl],
                            v=v_ref[bi, pl.ds(rows, CHUNK), sl], cs=cs,
                            grow=gt[d * npairs + hp:d * npairs + hp + 1, :], c0=d * N_HEADS + 2 * hp,
                            sidx=(bi * 2 + d) * npairs + hp, lower=lower,
                            dst=(o_ref, bi, rows, sl)))
            rounds.append(items)
        _delta_prep([it for items in rounds for it in items])
        for items in rounds:
            outs = _delta_scan(items, s_ref)
            for it, o in zip(items, outs):
                o_ref, bi, rows, sl = it["dst"]
                o_ref[bi, pl.ds(rows, CHUNK), sl] = o
        return carry

    lax.fori_loop(0, nc // DELTA_CHUNKS_PER_ITER, body, 0)


def _delta(q, k, v, cs, gt):
    b, s, _ = q.shape
    tc = TC_DELTA
    nb = s // tc
    ncb = tc // CHUNK
    fwd = lambda w: pl.BlockSpec((b, tc, w), lambda j: (0, j, 0))
    bwd = lambda w: pl.BlockSpec((b, tc, w), lambda j: (0, nb - 1 - j, 0))
    gt_f = pl.BlockSpec((b, ncb, N_HEADS, 2 * CHUNK), lambda j: (0, j, 0, 0))
    gt_b = pl.BlockSpec((b, ncb, N_HEADS, 2 * CHUNK), lambda j: (0, nb - 1 - j, 0, 0))
    return pl.pallas_call(
        _delta_kernel,
        grid=(nb,),
        in_specs=[fwd(D_DN), fwd(D_DN), fwd(D_DN), fwd(LANES), gt_f,
                  bwd(D_DN), bwd(D_DN), bwd(D_DN), bwd(LANES), gt_b],
        out_specs=[fwd(D_DN), bwd(D_DN)],
        out_shape=[jax.ShapeDtypeStruct((b, s, D_DN), F32), jax.ShapeDtypeStruct((b, s, D_DN), F32)],
        scratch_shapes=[pltpu.VMEM((b * N_HEADS, HEAD_DIM, PAIR), F32)],
        compiler_params=pltpu.CompilerParams(dimension_semantics=("arbitrary",),
                                             vmem_limit_bytes=VMEM_LIMIT),
        name="delta",
    )(q, k, v, cs, gt, q, k, v, cs, gt)


def _out_kernel(x_ref, p_ref, of_ref, ob_ref, z_ref, ya_ref, gates_ref,
                gdn_ref, wa_ref, wdn_ref, wo_ref, gffn_ref, w1_ref, w2_ref,
                gple_ref, wpg_ref, wpp_ref, gfin_ref, out_ref):
    o = of_ref[...] + ob_ref[...]
    z = z_ref[...].astype(F32)
    gdn = gdn_ref[...]
    parts = []
    for h in range(N_HEADS):
        sl = slice(h * HEAD_DIM, (h + 1) * HEAD_DIM)
        parts.append(_rms(o[:, sl], gdn) * _silu(z[:, sl]))
    y_dn = jnp.concatenate(parts, axis=1).astype(BF16)
    gates = gates_ref[...].astype(F32)
    merged = (jax.nn.sigmoid(gates[:, :D_MODEL]) * _dot(ya_ref[...], wa_ref[...])
              + jax.nn.sigmoid(gates[:, D_MODEL:]) * _dot(y_dn, wdn_ref[...]))
    h1 = x_ref[...] + _dot(merged.astype(BF16), wo_ref[...])

    n2 = _rms(h1, gffn_ref[...]).astype(BF16)
    h2 = h1
    for c in range(D_FF // FF_CHUNK):
        sl = slice(c * FF_CHUNK, (c + 1) * FF_CHUNK)
        a = jnp.maximum(_dot(n2, w1_ref[:, sl]), 0.0)
        h2 = h2 + _dot((a * a).astype(BF16), w2_ref[sl, :])

    n3 = _rms(h2, gple_ref[...]).astype(BF16)
    gate_p = jax.nn.sigmoid(_dot(n3, wpg_ref[...]))
    h3 = h2 + gate_p * _dot(p_ref[...].astype(BF16), wpp_ref[...])
    out_ref[...] = _rms(h3, gfin_ref[...])


def _out(x2, p2, o_f, o_b, z, ya, gates, gdn, wa, wdn, wo, gffn, w1, w2, gple, wpg, wpp, gfin):
    t = x2.shape[0]
    tm = TM_OUT
    row = lambda w: pl.BlockSpec((tm, w), lambda i: (i, 0))
    weights = (gdn, wa, wdn, wo, gffn, w1, w2, gple, wpg, wpp, gfin)
    return pl.pallas_call(
        _out_kernel,
        grid=(t // tm,),
        in_specs=[row(D_MODEL), row(PLE_DIM), row(D_DN), row(D_DN), row(D_DN), row(D_CONV),
                  row(2 * D_MODEL)] + [_resident(w.shape) for w in weights],
        out_specs=row(D_MODEL),
        out_shape=jax.ShapeDtypeStruct((t, D_MODEL), F32),
        compiler_params=pltpu.CompilerParams(dimension_semantics=("arbitrary",),
                                             vmem_limit_bytes=VMEM_LIMIT),
        name="out",
    )(x2, p2, o_f, o_b, z, ya, gates, *weights)


def kernel(x, p, g_mix, w_in, conv_a, conv_qkv, a_log_f, a_log_b, dt_bias_f, dt_bias_b, g_dn_norm,
           w_a_out, w_dn_out, w_o, g_ffn, w_ff1, w_ff2, g_ple, w_ple_gate, w_ple_proj, g_final):
    b, s, d = x.shape
    depth = p.shape[0]
    assert depth == 1 and d == D_MODEL and s % TC_DELTA == 0 and s % TS_FRONT == 0
    t = b * s
    i = 0

    w = w_in[i].astype(BF16)
    wg = w[:, W_GATES:]

    zeros8 = jnp.zeros((2 * N_HEADS,), F32)
    tail = jnp.zeros((LANES - 4 * N_HEADS,), F32)
    alog = jnp.concatenate([zeros8, a_log_f[i], a_log_b[i], tail])[None, :]
    dtb = jnp.concatenate([zeros8, dt_bias_f[i], dt_bias_b[i], tail])[None, :]
    out_weights = (w_a_out[i], w_dn_out[i], w_o[i], w_ff1[i], w_ff2[i], w_ple_gate[i])
    ya, q, k, v, cs, gt, z, gates, *out_weights = _front(x, g_mix[i][None, :], w, wg,
                                                         conv_a[i], conv_qkv[i], alog, dtb, out_weights)
    wa, wdn, wo, w1, w2, wpg = out_weights
    wpp = w_ple_proj[i].astype(BF16)

    gt = gt.reshape(b, s // CHUNK, N_HEADS, 2 * CHUNK)
    o_f, o_b = _delta(q, k, v, cs, gt)

    out = _out(x.reshape(t, d), p[i].reshape(t, PLE_DIM), o_f.reshape(t, D_DN), o_b.reshape(t, D_DN),
               z.reshape(t, D_DN), ya.reshape(t, D_CONV), gates.reshape(t, 2 * D_MODEL),
               g_dn_norm[i][None, :], wa, wdn, wo, g_ffn[i][None, :], w1, w2,
               g_ple[i][None, :], wpg, wpp, g_final[None, :])
    return out.reshape(b, s, d)
```

```python
import functools

import jax
import jax.numpy as jnp
from jax import lax
from jax.experimental import pallas as pl
from jax.experimental.pallas import tpu as pltpu

F32 = jnp.float32
BF16 = jnp.bfloat16

D_MODEL = 1024
PLE_DIM = 256
D_CONV = 512
N_HEADS = 4
HEAD_DIM = 128
D_DN = N_HEADS * HEAD_DIM
CHUNK = 64
D_FF = 4 * D_MODEL
EPS = 1e-6

LANES = 128
VMEM_LIMIT = 56 * 1024 * 1024

TS_FRONT = 512
TC_DELTA = 256
DELTA_CHUNKS_PER_ITER = 2
TM_OUT = 512
FF_CHUNK = 1024

W_PA = 0
W_QKV = W_PA + 3 * D_CONV
W_Z = W_QKV + 3 * D_DN
W_SM = W_Z + D_DN
W_GATES = W_SM + 4 * N_HEADS

CS_BETA, CS_G, CS_EG, CS_EKG, CS_EGT = 0, 8, 16, 24, 32


def _resident(shape):
    nd = len(shape)
    return pl.BlockSpec(shape, lambda *_: (0,) * nd, pipeline_mode=pl.Buffered(1))


def _rms(x, g):
    return x * lax.rsqrt(jnp.mean(x * x, axis=-1, keepdims=True) + EPS) * g


def _dot(a, b):
    return jnp.dot(a, b, preferred_element_type=F32)


def _dot_t(a, bt):
    return lax.dot_general(a, bt, (((1,), (1,)), ((), ())), preferred_element_type=F32)


def _silu(x):
    return x * jax.nn.sigmoid(x)


HALO = 8


def _conv3(ext, w, ts):
    n = ext.shape[0]
    up = pltpu.roll(ext, 1, 0)[HALO:HALO + ts]
    dn = pltpu.roll(ext, n - 1, 0)[HALO:HALO + ts]
    return up * w[0:1, :] + ext[HALO:HALO + ts] * w[1:2, :] + dn * w[2:3, :]


def _chunk_prefix_sums(g):
    ts = g.shape[0]
    hi = g.astype(BF16).astype(F32)
    r1 = g - hi
    mid = r1.astype(BF16).astype(F32)
    lo = (r1 - mid).astype(BF16).astype(F32)
    packed = (hi + pltpu.roll(mid, 16, 1) + pltpu.roll(lo, 32, 1)).astype(BF16)
    r = lax.broadcasted_iota(jnp.int32, (CHUNK, CHUNK), 0)
    c = lax.broadcasted_iota(jnp.int32, (CHUNK, CHUNK), 1)
    tri = jnp.where(c <= r, 1.0, 0.0).astype(BF16)
    res = jnp.concatenate([_dot(tri, packed[n * CHUNK:(n + 1) * CHUNK]) for n in range(ts // CHUNK)], axis=0)
    return res + pltpu.roll(res, LANES - 16, 1) + pltpu.roll(res, LANES - 32, 1)


def _front_kernel(x_ref, xp_ref, xn_ref, gmix_ref, wt_ref,
                  conv_a_ref, conv_qkv_ref, alog_ref, dtb_ref, *rest):
    nw = (len(rest) - 8) // 2
    wsrc_refs, (ya_ref, q_ref, k_ref, v_ref, cs_ref, gt_ref, z_ref, gates_ref), wdst_refs = (
        rest[:nw], rest[nw:nw + 8], rest[nw + 8:])
    j = pl.program_id(1)
    nj = pl.num_programs(1)
    ts = TS_FRONT
    has_prev = (j > 0).astype(F32)
    has_next = (j < nj - 1).astype(F32)
    inner = slice(HALO, HALO + ts)

    gm = gmix_ref[...]
    n_cur = _rms(x_ref[0], gm)
    n_ext = jnp.concatenate([_rms(xp_ref[0], gm) * has_prev, n_cur, _rms(xn_ref[0], gm) * has_next],
                            axis=0).astype(BF16)
    n_b = n_cur.astype(BF16)

    sm = _dot_t(n_b, wt_ref[W_SM:W_SM + LANES, :])
    lane = lax.broadcasted_iota(jnp.int32, sm.shape, 1)
    beta = jax.nn.sigmoid(sm)
    xs = sm + dtb_ref[...]
    softplus = jnp.maximum(xs, 0.0) + jnp.log1p(jnp.exp(-jnp.abs(xs)))
    g = jnp.where((lane >= 8) & (lane < 16), -jnp.exp(alog_ref[...]) * softplus, 0.0)

    pa = _dot_t(n_ext, wt_ref[W_PA:W_QKV, :])
    u = pa[:, :D_CONV] * pa[:, 2 * D_CONV:]
    ya = pa[inner, D_CONV:2 * D_CONV] * _conv3(u, conv_a_ref[...], ts)
    ya_ref[0] = ya.astype(ya_ref.dtype)

    pre = _chunk_prefix_sums(g)
    pre3 = pre.reshape(ts // CHUNK, CHUNK, LANES)
    tot = jnp.broadcast_to(pre3[:, CHUNK - 1:CHUNK, :], pre3.shape).reshape(ts, LANES)
    gc = jnp.where(lane < 12, pre, tot - pre + g)
    eg = jnp.exp(gc)
    ekg = jnp.exp(tot - gc)
    egt = jnp.exp(tot)

    g8 = pltpu.roll(gc, LANES - 8, 1)
    cs = jnp.where(lane < 8, beta, 0.0)
    cs = jnp.where((lane >= 8) & (lane < 16), gc, cs)
    cs = jnp.where((lane >= 16) & (lane < 24), pltpu.roll(eg, 8, 1), cs)
    cs = jnp.where((lane >= 24) & (lane < 32), pltpu.roll(ekg, 16, 1), cs)
    cs = jnp.where((lane >= 32) & (lane < 40), pltpu.roll(egt, 24, 1), cs)
    cs_ref[0] = cs

    g8t = g8.T
    for n in range(ts // CHUNK):
        gt_ref[0, n] = g8t[0:8, n * CHUNK:(n + 1) * CHUNK]

    for gi, (ref, scale) in enumerate(((q_ref, HEAD_DIM ** -0.5), (k_ref, None), (v_ref, None))):
        sl = slice(gi * D_DN, (gi + 1) * D_DN)
        c = _silu(_conv3(_dot_t(n_ext, wt_ref[W_QKV + gi * D_DN:W_QKV + (gi + 1) * D_DN, :]), conv_qkv_ref[:, sl], ts))
        if ref is v_ref:
            v_ref[0] = c
            continue
        for h in range(N_HEADS):
            hs = slice(h * HEAD_DIM, (h + 1) * HEAD_DIM)
            ch = c[:, hs]
            cn = ch * lax.rsqrt(jnp.sum(ch * ch, axis=-1, keepdims=True) + EPS)
            ref[0, :, hs] = cn if scale is None else cn * scale

    z_ref[0] = _dot_t(n_b, wt_ref[W_Z:W_SM, :]).astype(z_ref.dtype)
    gates_ref[0] = _dot_t(n_b, wt_ref[W_GATES:, :]).astype(gates_ref.dtype)

    for src, dst in zip(wsrc_refs, wdst_refs):
        dst[...] = src[...].astype(dst.dtype)


def _front(x, g_mix, wt, conv_a, conv_qkv, alog, dtb, cast_weights):
    b, s, d = x.shape
    ts = TS_FRONT
    nj = s // ts
    r8 = ts // HALO
    n8 = s // HALO

    def cur(w):
        return pl.BlockSpec((1, ts, w), lambda i, j: (i, j, 0))

    prev = pl.BlockSpec((1, HALO, d), lambda i, j: (i, jnp.maximum(j * r8 - 1, 0), 0))
    nxt = pl.BlockSpec((1, HALO, d), lambda i, j: (i, jnp.minimum((j + 1) * r8, n8 - 1), 0))
    consts = (g_mix, wt, conv_a, conv_qkv, alog, dtb)
    slab = lambda a: pl.BlockSpec((a.shape[0] // (b * nj), a.shape[1]), lambda i, j: (i * nj + j, 0))
    return pl.pallas_call(
        _front_kernel,
        grid=(b, nj),
        in_specs=[cur(d), prev, nxt] + [_resident(c.shape) for c in consts] + [slab(a) for a in cast_weights],
        out_specs=[cur(D_CONV), cur(D_DN), cur(D_DN), cur(D_DN), cur(LANES),
                   pl.BlockSpec((1, ts // CHUNK, 8, CHUNK), lambda i, j: (i, j, 0, 0)),
                   cur(D_DN), cur(2 * D_MODEL)] + [slab(a) for a in cast_weights],
        out_shape=[jax.ShapeDtypeStruct((b, s, D_CONV), BF16),
                   jax.ShapeDtypeStruct((b, s, D_DN), F32),
                   jax.ShapeDtypeStruct((b, s, D_DN), F32),
                   jax.ShapeDtypeStruct((b, s, D_DN), F32),
                   jax.ShapeDtypeStruct((b, s, LANES), F32),
                   jax.ShapeDtypeStruct((b, s // CHUNK, 8, CHUNK), F32),
                   jax.ShapeDtypeStruct((b, s, D_DN), BF16),
                   jax.ShapeDtypeStruct((b, s, 2 * D_MODEL), BF16)]
        + [jax.ShapeDtypeStruct(a.shape, BF16) for a in cast_weights],
        compiler_params=pltpu.CompilerParams(dimension_semantics=("arbitrary", "arbitrary"),
                                             vmem_limit_bytes=VMEM_LIMIT),
        name="front",
    )(x, x, x, *consts, *cast_weights)


PAIR = 2 * HEAD_DIM


def _block_diag(x, half):
    lane = lax.broadcasted_iota(jnp.int32, x.shape, 1)
    zero = jnp.zeros_like(x)
    return jnp.concatenate([jnp.where(lane < half, x, zero), jnp.where(lane >= half, x, zero)], axis=0)


def _delta_prep(items):
    r = lax.broadcasted_iota(jnp.int32, (CHUNK, 2 * CHUNK), 0)
    lane = lax.broadcasted_iota(jnp.int32, (CHUNK, 2 * CHUNK), 1)
    c = jnp.where(lane < CHUNK, lane, lane - CHUNK)
    first = lane < CHUNK
    eye = jnp.where(r == c, 1.0, 0.0)

    def col256(cs, base, c0):
        return jnp.concatenate([jnp.broadcast_to(cs[:, base + c0:base + c0 + 1], (CHUNK, HEAD_DIM)),
                                jnp.broadcast_to(cs[:, base + c0 + 1:base + c0 + 2], (CHUNK, HEAD_DIM))], axis=1)

    for it in items:
        cs, c0 = it["cs"], it["c0"]
        it["beta"] = col256(cs, CS_BETA, c0)
        it["eg"] = col256(cs, CS_EG, c0)
        it["kb"] = it["k"] * it["beta"]
        lhs = jnp.concatenate([it["kb"], it["q"]], axis=0).astype(BF16)
        it["kq"] = lax.dot_general(lhs, _block_diag(it["k"].astype(BF16), HEAD_DIM),
                                   (((1,), (1,)), ((), ())), preferred_element_type=F32)

    for it in items:
        cs, c0 = it["cs"], it["c0"]
        incl = (r >= c) if it["lower"] else (r <= c)
        strict = (r > c) if it["lower"] else (r < c)
        gcol = jnp.where(first, cs[:, CS_G + c0:CS_G + c0 + 1], cs[:, CS_G + c0 + 1:CS_G + c0 + 2])
        decay = jnp.where(incl, jnp.exp(jnp.where(incl, gcol - it["grow"], 0.0)), 0.0)
        a = jnp.where(strict, it["kq"][:CHUNK] * decay, 0.0)
        it["intra"] = (it["kq"][CHUNK:] * decay).astype(BF16)
        it["p"] = eye - a
        it["a"] = a.astype(BF16)

    for it in items:
        it["y"] = _dot(it["a"], _block_diag(it["a"], CHUNK)).astype(BF16)
    steps = 2
    while 2 * steps < CHUNK:
        steps *= 2
        for it in items:
            res = _dot(jnp.concatenate([it["p"].astype(BF16), it["y"]], axis=0), _block_diag(it["y"], CHUNK))
            it["p"] = it["p"] + res[:CHUNK]
            it["y"] = res[CHUNK:].astype(BF16)
    for it in items:
        it["p"] = it["p"] + _dot(it["p"].astype(BF16), _block_diag(it["y"], CHUNK))

    for it in items:
        vb = (it["v"] * it["beta"]).astype(BF16)
        kbg = (it["kb"] * it["eg"]).astype(BF16)
        rhs = jnp.concatenate([_block_diag(vb, HEAD_DIM), _block_diag(kbg, HEAD_DIM)], axis=1)
        uw = _dot(it["p"].astype(BF16), rhs)
        it["u"] = uw[:, :PAIR]
        it["w"] = uw[:, PAIR:]
        it["qg"] = it["q"] * it["eg"]
        kg = (it["k"] * col256(it["cs"], CS_EKG, it["c0"])).astype(BF16)
        it["kgs"] = jnp.concatenate([kg[:, :HEAD_DIM], kg[:, HEAD_DIM:]], axis=0)


def _delta_scan(items, s_ref):
    for it in items:
        it["state"] = s_ref[it["sidx"]]
        sb = it["state"].astype(BF16)
        ws = []
        for h in range(2):
            sl = slice(h * HEAD_DIM, (h + 1) * HEAD_DIM)
            lhs = jnp.concatenate([it["w"][:, sl], it["qg"][:, sl]], axis=0).astype(BF16)
            ws.append(_dot(lhs, sb[:, sl]))
        it["ws_top"] = jnp.concatenate([ws[0][:CHUNK], ws[1][:CHUNK]], axis=1)
        it["ws_bot"] = jnp.concatenate([ws[0][CHUNK:], ws[1][CHUNK:]], axis=1)

    outs = []
    for it in items:
        cs, c0 = it["cs"], it["c0"]
        v_new = _block_diag((it["u"] - it["ws_top"]).astype(BF16), HEAD_DIM)
        outs.append(it["ws_bot"] + _dot(it["intra"], v_new))
        upd = lax.dot_general(it["kgs"], v_new, (((0,), (0,)), ((), ())), preferred_element_type=F32)
        egt = jnp.concatenate([jnp.broadcast_to(cs[0:1, CS_EGT + c0:CS_EGT + c0 + 1], (1, HEAD_DIM)),
                               jnp.broadcast_to(cs[0:1, CS_EGT + c0 + 1:CS_EGT + c0 + 2], (1, HEAD_DIM))], axis=1)
        s_ref[it["sidx"]] = it["state"] * egt + upd
    return outs


def _delta_kernel(qf_ref, kf_ref, vf_ref, csf_ref, gtf_ref, qb_ref, kb_ref, vb_ref, csb_ref, gtb_ref,
                  of_ref, ob_ref, s_ref):
    @pl.when(pl.program_id(0) == 0)
    def _():
        s_ref[...] = jnp.zeros_like(s_ref)

    nbatch = qf_ref.shape[0]
    nc = TC_DELTA // CHUNK
    npairs = N_HEADS // 2

    def body(i, carry):
        rounds = []
        for cc in range(DELTA_CHUNKS_PER_ITER):
            jf = DELTA_CHUNKS_PER_ITER * i + cc
            jb = nc - 1 - jf
            rf = pl.multiple_of(jf * CHUNK, CHUNK)
            rb = pl.multiple_of(jb * CHUNK, CHUNK)
            items = []
            for bi in range(nbatch):
                for lower, j, rows, (q_ref, k_ref, v_ref, cs_ref, gt_ref, o_ref) in (
                        (True, jf, rf, (qf_ref, kf_ref, vf_ref, csf_ref, gtf_ref, of_ref)),
                        (False, jb, rb, (qb_ref, kb_ref, vb_ref, csb_ref, gtb_ref, ob_ref))):
                    d = 0 if lower else 1
                    cs = cs_ref[bi, pl.ds(rows, CHUNK), :]
                    gt = gt_ref[bi, j]
                    for hp in range(npairs):
                        sl = slice(hp * PAIR, (hp + 1) * PAIR)
                        items.append(dict(
                            q=q_ref[bi, pl.ds(rows, CHUNK), sl], k=k_ref[bi, pl.ds(rows, CHUNK), sl],
                            v=v_ref[bi, pl.ds(rows, CHUNK), sl], cs=cs,
                            grow=gt[d * npairs + hp:d * npairs + hp + 1, :], c0=d * N_HEADS + 2 * hp,
                            sidx=(bi * 2 + d) * npairs + hp, lower=lower,
                            dst=(o_ref, bi, rows, sl)))
            rounds.append(items)
        _delta_prep([it for items in rounds for it in items])
        for items in rounds:
            outs = _delta_scan(items, s_ref)
            for it, o in zip(items, outs):
                o_ref, bi, rows, sl = it["dst"]
                o_ref[bi, pl.ds(rows, CHUNK), sl] = o
        return carry

    lax.fori_loop(0, nc // DELTA_CHUNKS_PER_ITER, body, 0)


def _delta(q, k, v, cs, gt):
    b, s, _ = q.shape
    tc = TC_DELTA
    nb = s // tc
    ncb = tc // CHUNK
    fwd = lambda w: pl.BlockSpec((b, tc, w), lambda j: (0, j, 0))
    bwd = lambda w: pl.BlockSpec((b, tc, w), lambda j: (0, nb - 1 - j, 0))
    gt_f = pl.BlockSpec((b, ncb, N_HEADS, 2 * CHUNK), lambda j: (0, j, 0, 0))
    gt_b = pl.BlockSpec((b, ncb, N_HEADS, 2 * CHUNK), lambda j: (0, nb - 1 - j, 0, 0))
    return pl.pallas_call(
        _delta_kernel,
        grid=(nb,),
        in_specs=[fwd(D_DN), fwd(D_DN), fwd(D_DN), fwd(LANES), gt_f,
                  bwd(D_DN), bwd(D_DN), bwd(D_DN), bwd(LANES), gt_b],
        out_specs=[fwd(D_DN), bwd(D_DN)],
        out_shape=[jax.ShapeDtypeStruct((b, s, D_DN), F32), jax.ShapeDtypeStruct((b, s, D_DN), F32)],
        scratch_shapes=[pltpu.VMEM((b * N_HEADS, HEAD_DIM, PAIR), F32)],
        compiler_params=pltpu.CompilerParams(dimension_semantics=("arbitrary",),
                                             vmem_limit_bytes=VMEM_LIMIT),
        name="delta",
    )(q, k, v, cs, gt, q, k, v, cs, gt)


def _out_kernel(x_ref, p_ref, of_ref, ob_ref, z_ref, ya_ref, gates_ref,
                gdn_ref, wa_ref, wdn_ref, wo_ref, gffn_ref, w1_ref, w2_ref,
                gple_ref, wpg_ref, wpp_ref, gfin_ref, out_ref):
    o = of_ref[...] + ob_ref[...]
    z = z_ref[...].astype(F32)
    gdn = gdn_ref[...]
    parts = []
    for h in range(N_HEADS):
        sl = slice(h * HEAD_DIM, (h + 1) * HEAD_DIM)
        parts.append(_rms(o[:, sl], gdn) * _silu(z[:, sl]))
    y_dn = jnp.concatenate(parts, axis=1).astype(BF16)
    gates = gates_ref[...].astype(F32)
    merged = (jax.nn.sigmoid(gates[:, :D_MODEL]) * _dot(ya_ref[...], wa_ref[...])
              + jax.nn.sigmoid(gates[:, D_MODEL:]) * _dot(y_dn, wdn_ref[...]))
    h1 = x_ref[...] + _dot(merged.astype(BF16), wo_ref[...])

    n2 = _rms(h1, gffn_ref[...]).astype(BF16)
    h2 = h1
    for c in range(D_FF // FF_CHUNK):
        sl = slice(c * FF_CHUNK, (c + 1) * FF_CHUNK)
        a = jnp.maximum(_dot(n2, w1_ref[:, sl]), 0.0)
        h2 = h2 + _dot((a * a).astype(BF16), w2_ref[sl, :])

    n3 = _rms(h2, gple_ref[...]).astype(BF16)
    gate_p = jax.nn.sigmoid(_dot(n3, wpg_ref[...]))
    h3 = h2 + gate_p * _dot(p_ref[...].astype(BF16), wpp_ref[...])
    out_ref[...] = _rms(h3, gfin_ref[...])


def _out(x2, p2, o_f, o_b, z, ya, gates, gdn, wa, wdn, wo, gffn, w1, w2, gple, wpg, wpp, gfin):
    t = x2.shape[0]
    tm = TM_OUT
    row = lambda w: pl.BlockSpec((tm, w), lambda i: (i, 0))
    weights = (gdn, wa, wdn, wo, gffn, w1, w2, gple, wpg, wpp, gfin)
    return pl.pallas_call(
        _out_kernel,
        grid=(t // tm,),
        in_specs=[row(D_MODEL), row(PLE_DIM), row(D_DN), row(D_DN), row(D_DN), row(D_CONV),
                  row(2 * D_MODEL)] + [_resident(w.shape) for w in weights],
        out_specs=row(D_MODEL),
        out_shape=jax.ShapeDtypeStruct((t, D_MODEL), F32),
        compiler_params=pltpu.CompilerParams(dimension_semantics=("arbitrary",),
                                             vmem_limit_bytes=VMEM_LIMIT),
        name="out",
    )(x2, p2, o_f, o_b, z, ya, gates, *weights)


def kernel(x, p, g_mix, w_in, conv_a, conv_qkv, a_log_f, a_log_b, dt_bias_f, dt_bias_b, g_dn_norm,
           w_a_out, w_dn_out, w_o, g_ffn, w_ff1, w_ff2, g_ple, w_ple_gate, w_ple_proj, g_final):
    b, s, d = x.shape
    depth = p.shape[0]
    assert depth == 1 and d == D_MODEL and s % TC_DELTA == 0 and s % TS_FRONT == 0
    t = b * s
    i = 0

    wt = jnp.transpose(w_in[i]).astype(BF16)

    zeros8 = jnp.zeros((2 * N_HEADS,), F32)
    tail = jnp.zeros((LANES - 4 * N_HEADS,), F32)
    alog = jnp.concatenate([zeros8, a_log_f[i], a_log_b[i], tail])[None, :]
    dtb = jnp.concatenate([zeros8, dt_bias_f[i], dt_bias_b[i], tail])[None, :]
    out_weights = (w_a_out[i], w_dn_out[i], w_o[i], w_ff1[i], w_ff2[i], w_ple_gate[i])
    ya, q, k, v, cs, gt, z, gates, *out_weights = _front(x, g_mix[i][None, :], wt,
                                                         conv_a[i], conv_qkv[i], alog, dtb, out_weights)
    wa, wdn, wo, w1, w2, wpg = out_weights
    wpp = w_ple_proj[i].astype(BF16)

    gt = gt.reshape(b, s // CHUNK, N_HEADS, 2 * CHUNK)
    o_f, o_b = _delta(q, k, v, cs, gt)

    out = _out(x.reshape(t, d), p[i].reshape(t, PLE_DIM), o_f.reshape(t, D_DN), o_b.reshape(t, D_DN),
               z.reshape(t, D_DN), ya.reshape(t, D_CONV), gates.reshape(t, 2 * D_MODEL),
               g_dn_norm[i][None, :], wa, wdn, wo, g_ffn[i][None, :], w1, w2,
               g_ple[i][None, :], wpg, wpp, g_final[None, :])
    return out.reshape(b, s, d)
```

```python
import functools

import jax
import jax.numpy as jnp
from jax import lax
from jax.experimental import pallas as pl
from jax.experimental.pallas import tpu as pltpu

F32 = jnp.float32
BF16 = jnp.bfloat16

D_MODEL = 1024
PLE_DIM = 256
D_CONV = 512
N_HEADS = 4
HEAD_DIM = 128
D_DN = N_HEADS * HEAD_DIM
CHUNK = 64
D_FF = 4 * D_MODEL
EPS = 1e-6

LANES = 128
VMEM_LIMIT = 56 * 1024 * 1024

TS_FRONT = 512
TC_DELTA = 256
DELTA_CHUNKS_PER_ITER = 2
TM_OUT = 512
FF_CHUNK = 1024

W_PA = 0
W_QKV = W_PA + 3 * D_CONV
W_Z = W_QKV + 3 * D_DN
W_SM = W_Z + D_DN
W_GATES = W_SM + 4 * N_HEADS
W_END = W_GATES + 2 * D_MODEL
W_GROUPS = ((W_PA, W_QKV), (W_QKV, W_Z), (W_Z, W_SM), (W_SM, W_SM + LANES), (W_GATES, W_END))

CS_BETA, CS_G, CS_EG, CS_EKG, CS_EGT = 0, 8, 16, 24, 32


def _resident(shape):
    nd = len(shape)
    return pl.BlockSpec(shape, lambda *_: (0,) * nd, pipeline_mode=pl.Buffered(1))


def _rms(x, g):
    return x * lax.rsqrt(jnp.mean(x * x, axis=-1, keepdims=True) + EPS) * g


def _dot(a, b):
    return jnp.dot(a, b, preferred_element_type=F32)


def _silu(x):
    return x * jax.nn.sigmoid(x)


HALO = 8


def _conv3(ext, w, ts):
    n = ext.shape[0]
    up = pltpu.roll(ext, 1, 0)[HALO:HALO + ts]
    dn = pltpu.roll(ext, n - 1, 0)[HALO:HALO + ts]
    return up * w[0:1, :] + ext[HALO:HALO + ts] * w[1:2, :] + dn * w[2:3, :]


def _chunk_prefix_sums(g):
    ts = g.shape[0]
    hi = g.astype(BF16).astype(F32)
    r1 = g - hi
    mid = r1.astype(BF16).astype(F32)
    lo = (r1 - mid).astype(BF16).astype(F32)
    packed = (hi + pltpu.roll(mid, 16, 1) + pltpu.roll(lo, 32, 1)).astype(BF16)
    r = lax.broadcasted_iota(jnp.int32, (CHUNK, CHUNK), 0)
    c = lax.broadcasted_iota(jnp.int32, (CHUNK, CHUNK), 1)
    tri = jnp.where(c <= r, 1.0, 0.0).astype(BF16)
    res = jnp.concatenate([_dot(tri, packed[n * CHUNK:(n + 1) * CHUNK]) for n in range(ts // CHUNK)], axis=0)
    return res + pltpu.roll(res, LANES - 16, 1) + pltpu.roll(res, LANES - 32, 1)


def _front_kernel(x_ref, xp_ref, xn_ref, gmix_ref, wt_ref,
                  conv_a_ref, conv_qkv_ref, alog_ref, dtb_ref, *rest):
    nw = (len(rest) - 8 - len(W_GROUPS)) // 2
    wsrc_refs, (ya_ref, q_ref, k_ref, v_ref, cs_ref, gt_ref, z_ref, gates_ref), wdst_refs = (
        rest[:nw], rest[nw:nw + 8], rest[nw + 8:2 * nw + 8])
    wpa_ref, wqkv_ref, wz_ref, wsm_ref, wg_ref = w_refs = rest[2 * nw + 8:]
    j = pl.program_id(1)
    nj = pl.num_programs(1)
    ts = TS_FRONT

    @pl.when((pl.program_id(0) == 0) & (j == 0))
    def _():
        for (lo, hi), dst in zip(W_GROUPS, w_refs):
            dst[...] = wt_ref[lo:hi, :].T

    has_prev = (j > 0).astype(F32)
    has_next = (j < nj - 1).astype(F32)
    inner = slice(HALO, HALO + ts)

    gm = gmix_ref[...]
    n_cur = _rms(x_ref[0], gm)
    n_ext = jnp.concatenate([_rms(xp_ref[0], gm) * has_prev, n_cur, _rms(xn_ref[0], gm) * has_next],
                            axis=0).astype(BF16)
    n_b = n_cur.astype(BF16)

    sm = _dot(n_b, wsm_ref[...])
    lane = lax.broadcasted_iota(jnp.int32, sm.shape, 1)
    beta = jax.nn.sigmoid(sm)
    xs = sm + dtb_ref[...]
    softplus = jnp.maximum(xs, 0.0) + jnp.log1p(jnp.exp(-jnp.abs(xs)))
    g = jnp.where((lane >= 8) & (lane < 16), -jnp.exp(alog_ref[...]) * softplus, 0.0)

    pa = _dot(n_ext, wpa_ref[...])
    u = pa[:, :D_CONV] * pa[:, 2 * D_CONV:]
    ya = pa[inner, D_CONV:2 * D_CONV] * _conv3(u, conv_a_ref[...], ts)
    ya_ref[0] = ya.astype(ya_ref.dtype)

    pre = _chunk_prefix_sums(g)
    pre3 = pre.reshape(ts // CHUNK, CHUNK, LANES)
    tot = jnp.broadcast_to(pre3[:, CHUNK - 1:CHUNK, :], pre3.shape).reshape(ts, LANES)
    gc = jnp.where(lane < 12, pre, tot - pre + g)
    eg = jnp.exp(gc)
    ekg = jnp.exp(tot - gc)
    egt = jnp.exp(tot)

    g8 = pltpu.roll(gc, LANES - 8, 1)
    cs = jnp.where(lane < 8, beta, 0.0)
    cs = jnp.where((lane >= 8) & (lane < 16), gc, cs)
    cs = jnp.where((lane >= 16) & (lane < 24), pltpu.roll(eg, 8, 1), cs)
    cs = jnp.where((lane >= 24) & (lane < 32), pltpu.roll(ekg, 16, 1), cs)
    cs = jnp.where((lane >= 32) & (lane < 40), pltpu.roll(egt, 24, 1), cs)
    cs_ref[0] = cs

    g8t = g8.T
    for n in range(ts // CHUNK):
        gt_ref[0, n] = g8t[0:8, n * CHUNK:(n + 1) * CHUNK]

    for gi, (ref, scale) in enumerate(((q_ref, HEAD_DIM ** -0.5), (k_ref, None), (v_ref, None))):
        sl = slice(gi * D_DN, (gi + 1) * D_DN)
        c = _silu(_conv3(_dot(n_ext, wqkv_ref[:, sl]), conv_qkv_ref[:, sl], ts))
        if ref is v_ref:
            v_ref[0] = c
            continue
        for h in range(N_HEADS):
            hs = slice(h * HEAD_DIM, (h + 1) * HEAD_DIM)
            ch = c[:, hs]
            cn = ch * lax.rsqrt(jnp.sum(ch * ch, axis=-1, keepdims=True) + EPS)
            ref[0, :, hs] = cn if scale is None else cn * scale

    z_ref[0] = _dot(n_b, wz_ref[...]).astype(z_ref.dtype)
    gates_ref[0] = _dot(n_b, wg_ref[...]).astype(gates_ref.dtype)

    for src, dst in zip(wsrc_refs, wdst_refs):
        dst[...] = src[...].astype(dst.dtype)


def _front(x, g_mix, wt, conv_a, conv_qkv, alog, dtb, cast_weights):
    b, s, d = x.shape
    ts = TS_FRONT
    nj = s // ts
    r8 = ts // HALO
    n8 = s // HALO

    def cur(w):
        return pl.BlockSpec((1, ts, w), lambda i, j: (i, j, 0))

    prev = pl.BlockSpec((1, HALO, d), lambda i, j: (i, jnp.maximum(j * r8 - 1, 0), 0))
    nxt = pl.BlockSpec((1, HALO, d), lambda i, j: (i, jnp.minimum((j + 1) * r8, n8 - 1), 0))
    consts = (g_mix, wt, conv_a, conv_qkv, alog, dtb)
    slab = lambda a: pl.BlockSpec((a.shape[0] // (b * nj), a.shape[1]), lambda i, j: (i * nj + j, 0))
    return pl.pallas_call(
        _front_kernel,
        grid=(b, nj),
        in_specs=[cur(d), prev, nxt] + [_resident(c.shape) for c in consts] + [slab(a) for a in cast_weights],
        out_specs=[cur(D_CONV), cur(D_DN), cur(D_DN), cur(D_DN), cur(LANES),
                   pl.BlockSpec((1, ts // CHUNK, 8, CHUNK), lambda i, j: (i, j, 0, 0)),
                   cur(D_DN), cur(2 * D_MODEL)] + [slab(a) for a in cast_weights],
        out_shape=[jax.ShapeDtypeStruct((b, s, D_CONV), BF16),
                   jax.ShapeDtypeStruct((b, s, D_DN), F32),
                   jax.ShapeDtypeStruct((b, s, D_DN), F32),
                   jax.ShapeDtypeStruct((b, s, D_DN), F32),
                   jax.ShapeDtypeStruct((b, s, LANES), F32),
                   jax.ShapeDtypeStruct((b, s // CHUNK, 8, CHUNK), F32),
                   jax.ShapeDtypeStruct((b, s, D_DN), BF16),
                   jax.ShapeDtypeStruct((b, s, 2 * D_MODEL), BF16)]
        + [jax.ShapeDtypeStruct(a.shape, BF16) for a in cast_weights],
        scratch_shapes=[pltpu.VMEM((d, hi - lo), BF16) for lo, hi in W_GROUPS],
        compiler_params=pltpu.CompilerParams(dimension_semantics=("arbitrary", "arbitrary"),
                                             vmem_limit_bytes=VMEM_LIMIT),
        name="front",
    )(x, x, x, *consts, *cast_weights)


PAIR = 2 * HEAD_DIM


def _block_diag(x, half):
    lane = lax.broadcasted_iota(jnp.int32, x.shape, 1)
    zero = jnp.zeros_like(x)
    return jnp.concatenate([jnp.where(lane < half, x, zero), jnp.where(lane >= half, x, zero)], axis=0)


def _delta_prep(items):
    r = lax.broadcasted_iota(jnp.int32, (CHUNK, 2 * CHUNK), 0)
    lane = lax.broadcasted_iota(jnp.int32, (CHUNK, 2 * CHUNK), 1)
    c = jnp.where(lane < CHUNK, lane, lane - CHUNK)
    first = lane < CHUNK
    eye = jnp.where(r == c, 1.0, 0.0)

    def col256(cs, base, c0):
        return jnp.concatenate([jnp.broadcast_to(cs[:, base + c0:base + c0 + 1], (CHUNK, HEAD_DIM)),
                                jnp.broadcast_to(cs[:, base + c0 + 1:base + c0 + 2], (CHUNK, HEAD_DIM))], axis=1)

    for it in items:
        cs, c0 = it["cs"], it["c0"]
        it["beta"] = col256(cs, CS_BETA, c0)
        it["eg"] = col256(cs, CS_EG, c0)
        it["kb"] = it["k"] * it["beta"]
        lhs = jnp.concatenate([it["kb"], it["q"]], axis=0).astype(BF16)
        it["kq"] = lax.dot_general(lhs, _block_diag(it["k"].astype(BF16), HEAD_DIM),
                                   (((1,), (1,)), ((), ())), preferred_element_type=F32)

    for it in items:
        cs, c0 = it["cs"], it["c0"]
        incl = (r >= c) if it["lower"] else (r <= c)
        strict = (r > c) if it["lower"] else (r < c)
        gcol = jnp.where(first, cs[:, CS_G + c0:CS_G + c0 + 1], cs[:, CS_G + c0 + 1:CS_G + c0 + 2])
        decay = jnp.where(incl, jnp.exp(jnp.where(incl, gcol - it["grow"], 0.0)), 0.0)
        a = jnp.where(strict, it["kq"][:CHUNK] * decay, 0.0)
        it["intra"] = (it["kq"][CHUNK:] * decay).astype(BF16)
        it["p"] = eye - a
        it["a"] = a.astype(BF16)

    for it in items:
        it["y"] = _dot(it["a"], _block_diag(it["a"], CHUNK)).astype(BF16)
    steps = 2
    while 2 * steps < CHUNK:
        steps *= 2
        for it in items:
            res = _dot(jnp.concatenate([it["p"].astype(BF16), it["y"]], axis=0), _block_diag(it["y"], CHUNK))
            it["p"] = it["p"] + res[:CHUNK]
            it["y"] = res[CHUNK:].astype(BF16)
    for it in items:
        it["p"] = it["p"] + _dot(it["p"].astype(BF16), _block_diag(it["y"], CHUNK))

    for it in items:
        vb = (it["v"] * it["beta"]).astype(BF16)
        kbg = (it["kb"] * it["eg"]).astype(BF16)
        rhs = jnp.concatenate([_block_diag(vb, HEAD_DIM), _block_diag(kbg, HEAD_DIM)], axis=1)
        uw = _dot(it["p"].astype(BF16), rhs)
        it["u"] = uw[:, :PAIR]
        it["w"] = uw[:, PAIR:]
        it["qg"] = it["q"] * it["eg"]
        kg = (it["k"] * col256(it["cs"], CS_EKG, it["c0"])).astype(BF16)
        it["kgs"] = jnp.concatenate([kg[:, :HEAD_DIM], kg[:, HEAD_DIM:]], axis=0)


def _delta_scan(items, s_ref):
    for it in items:
        it["state"] = s_ref[it["sidx"]]
        sb = it["state"].astype(BF16)
        ws = []
        for h in range(2):
            sl = slice(h * HEAD_DIM, (h + 1) * HEAD_DIM)
            lhs = jnp.concatenate([it["w"][:, sl], it["qg"][:, sl]], axis=0).astype(BF16)
            ws.append(_dot(lhs, sb[:, sl]))
        it["ws_top"] = jnp.concatenate([ws[0][:CHUNK], ws[1][:CHUNK]], axis=1)
        it["ws_bot"] = jnp.concatenate([ws[0][CHUNK:], ws[1][CHUNK:]], axis=1)

    outs = []
    for it in items:
        cs, c0 = it["cs"], it["c0"]
        v_new = _block_diag((it["u"] - it["ws_top"]).astype(BF16), HEAD_DIM)
        outs.append(it["ws_bot"] + _dot(it["intra"], v_new))
        upd = lax.dot_general(it["kgs"], v_new, (((0,), (0,)), ((), ())), preferred_element_type=F32)
        egt = jnp.concatenate([jnp.broadcast_to(cs[0:1, CS_EGT + c0:CS_EGT + c0 + 1], (1, HEAD_DIM)),
                               jnp.broadcast_to(cs[0:1, CS_EGT + c0 + 1:CS_EGT + c0 + 2], (1, HEAD_DIM))], axis=1)
        s_ref[it["sidx"]] = it["state"] * egt + upd
    return outs


def _delta_kernel(qf_ref, kf_ref, vf_ref, csf_ref, gtf_ref, qb_ref, kb_ref, vb_ref, csb_ref, gtb_ref,
                  of_ref, ob_ref, s_ref):
    @pl.when(pl.program_id(0) == 0)
    def _():
        s_ref[...] = jnp.zeros_like(s_ref)

    nbatch = qf_ref.shape[0]
    nc = TC_DELTA // CHUNK
    npairs = N_HEADS // 2

    def body(i, carry):
        rounds = []
        for cc in range(DELTA_CHUNKS_PER_ITER):
            jf = DELTA_CHUNKS_PER_ITER * i + cc
            jb = nc - 1 - jf
            rf = pl.multiple_of(jf * CHUNK, CHUNK)
            rb = pl.multiple_of(jb * CHUNK, CHUNK)
            items = []
            for bi in range(nbatch):
                for lower, j, rows, (q_ref, k_ref, v_ref, cs_ref, gt_ref, o_ref) in (
                        (True, jf, rf, (qf_ref, kf_ref, vf_ref, csf_ref, gtf_ref, of_ref)),
                        (False, jb, rb, (qb_ref, kb_ref, vb_ref, csb_ref, gtb_ref, ob_ref))):
                    d = 0 if lower else 1
                    cs = cs_ref[bi, pl.ds(rows, CHUNK), :]
                    gt = gt_ref[bi, j]
                    for hp in range(npairs):
                        sl = slice(hp * PAIR, (hp + 1) * PAIR)
                        items.append(dict(
                            q=q_ref[bi, pl.ds(rows, CHUNK), sl], k=k_ref[bi, pl.ds(rows, CHUNK), sl],
                            v=v_ref[bi, pl.ds(rows, CHUNK), sl], cs=cs,
                            grow=gt[d * npairs + hp:d * npairs + hp + 1, :], c0=d * N_HEADS + 2 * hp,
                            sidx=(bi * 2 + d) * npairs + hp, lower=lower,
                            dst=(o_ref, bi, rows, sl)))
            rounds.append(items)
        _delta_prep([it for items in rounds for it in items])
        for items in rounds:
            outs = _delta_scan(items, s_ref)
            for it, o in zip(items, outs):
                o_ref, bi, rows, sl = it["dst"]
                o_ref[bi, pl.ds(rows, CHUNK), sl] = o
        return carry

    lax.fori_loop(0, nc // DELTA_CHUNKS_PER_ITER, body, 0)


def _delta(q, k, v, cs, gt):
    b, s, _ = q.shape
    tc = TC_DELTA
    nb = s // tc
    ncb = tc // CHUNK
    fwd = lambda w: pl.BlockSpec((b, tc, w), lambda j: (0, j, 0))
    bwd = lambda w: pl.BlockSpec((b, tc, w), lambda j: (0, nb - 1 - j, 0))
    gt_f = pl.BlockSpec((b, ncb, N_HEADS, 2 * CHUNK), lambda j: (0, j, 0, 0))
    gt_b = pl.BlockSpec((b, ncb, N_HEADS, 2 * CHUNK), lambda j: (0, nb - 1 - j, 0, 0))
    return pl.pallas_call(
        _delta_kernel,
        grid=(nb,),
        in_specs=[fwd(D_DN), fwd(D_DN), fwd(D_DN), fwd(LANES), gt_f,
                  bwd(D_DN), bwd(D_DN), bwd(D_DN), bwd(LANES), gt_b],
        out_specs=[fwd(D_DN), bwd(D_DN)],
        out_shape=[jax.ShapeDtypeStruct((b, s, D_DN), F32), jax.ShapeDtypeStruct((b, s, D_DN), F32)],
        scratch_shapes=[pltpu.VMEM((b * N_HEADS, HEAD_DIM, PAIR), F32)],
        compiler_params=pltpu.CompilerParams(dimension_semantics=("arbitrary",),
                                             vmem_limit_bytes=VMEM_LIMIT),
        name="delta",
    )(q, k, v, cs, gt, q, k, v, cs, gt)


def _out_kernel(x_ref, p_ref, of_ref, ob_ref, z_ref, ya_ref, gates_ref,
                gdn_ref, wa_ref, wdn_ref, wo_ref, gffn_ref, w1_ref, w2_ref,
                gple_ref, wpg_ref, wpp_ref, gfin_ref, out_ref):
    o = of_ref[...] + ob_ref[...]
    z = z_ref[...].astype(F32)
    gdn = gdn_ref[...]
    parts = []
    for h in range(N_HEADS):
        sl = slice(h * HEAD_DIM, (h + 1) * HEAD_DIM)
        parts.append(_rms(o[:, sl], gdn) * _silu(z[:, sl]))
    y_dn = jnp.concatenate(parts, axis=1).astype(BF16)
    gates = gates_ref[...].astype(F32)
    merged = (jax.nn.sigmoid(gates[:, :D_MODEL]) * _dot(ya_ref[...], wa_ref[...])
              + jax.nn.sigmoid(gates[:, D_MODEL:]) * _dot(y_dn, wdn_ref[...]))
    h1 = x_ref[...] + _dot(merged.astype(BF16), wo_ref[...])

    n2 = _rms(h1, gffn_ref[...]).astype(BF16)
    h2 = h1
    for c in range(D_FF // FF_CHUNK):
        sl = slice(c * FF_CHUNK, (c + 1) * FF_CHUNK)
        a = jnp.maximum(_dot(n2, w1_ref[:, sl]), 0.0)
        h2 = h2 + _dot((a * a).astype(BF16), w2_ref[sl, :])

    n3 = _rms(h2, gple_ref[...]).astype(BF16)
    gate_p = jax.nn.sigmoid(_dot(n3, wpg_ref[...]))
    h3 = h2 + gate_p * _dot(p_ref[...].astype(BF16), wpp_ref[...])
    out_ref[...] = _rms(h3, gfin_ref[...])


def _out(x2, p2, o_f, o_b, z, ya, gates, gdn, wa, wdn, wo, gffn, w1, w2, gple, wpg, wpp, gfin):
    t = x2.shape[0]
    tm = TM_OUT
    row = lambda w: pl.BlockSpec((tm, w), lambda i: (i, 0))
    weights = (gdn, wa, wdn, wo, gffn, w1, w2, gple, wpg, wpp, gfin)
    return pl.pallas_call(
        _out_kernel,
        grid=(t // tm,),
        in_specs=[row(D_MODEL), row(PLE_DIM), row(D_DN), row(D_DN), row(D_DN), row(D_CONV),
                  row(2 * D_MODEL)] + [_resident(w.shape) for w in weights],
        out_specs=row(D_MODEL),
        out_shape=jax.ShapeDtypeStruct((t, D_MODEL), F32),
        compiler_params=pltpu.CompilerParams(dimension_semantics=("arbitrary",),
                                             vmem_limit_bytes=VMEM_LIMIT),
        name="out",
    )(x2, p2, o_f, o_b, z, ya, gates, *weights)


def kernel(x, p, g_mix, w_in, conv_a, conv_qkv, a_log_f, a_log_b, dt_bias_f, dt_bias_b, g_dn_norm,
           w_a_out, w_dn_out, w_o, g_ffn, w_ff1, w_ff2, g_ple, w_ple_gate, w_ple_proj, g_final):
    b, s, d = x.shape
    depth = p.shape[0]
    assert depth == 1 and d == D_MODEL and s % TC_DELTA == 0 and s % TS_FRONT == 0
    t = b * s
    i = 0

    wt = jnp.transpose(w_in[i]).astype(BF16)

    zeros8 = jnp.zeros((2 * N_HEADS,), F32)
    tail = jnp.zeros((LANES - 4 * N_HEADS,), F32)
    alog = jnp.concatenate([zeros8, a_log_f[i], a_log_b[i], tail])[None, :]
    dtb = jnp.concatenate([zeros8, dt_bias_f[i], dt_bias_b[i], tail])[None, :]
    out_weights = (w_a_out[i], w_dn_out[i], w_o[i], w_ff1[i], w_ff2[i], w_ple_gate[i])
    ya, q, k, v, cs, gt, z, gates, *out_weights = _front(x, g_mix[i][None, :], wt,
                                                         conv_a[i], conv_qkv[i], alog, dtb, out_weights)
    wa, wdn, wo, w1, w2, wpg = out_weights
    wpp = w_ple_proj[i].astype(BF16)

    gt = gt.reshape(b, s // CHUNK, N_HEADS, 2 * CHUNK)
    o_f, o_b = _delta(q, k, v, cs, gt)

    out = _out(x.reshape(t, d), p[i].reshape(t, PLE_DIM), o_f.reshape(t, D_DN), o_b.reshape(t, D_DN),
               z.reshape(t, D_DN), ya.reshape(t, D_CONV), gates.reshape(t, 2 * D_MODEL),
               g_dn_norm[i][None, :], wa, wdn, wo, g_ffn[i][None, :], w1, w2,
               g_ple[i][None, :], wpg, wpp, g_final[None, :])
    return out.reshape(b, s, d)
```

```python
import functools

import jax
import jax.numpy as jnp
from jax import lax
from jax.experimental import pallas as pl
from jax.experimental.pallas import tpu as pltpu

F32 = jnp.float32
BF16 = jnp.bfloat16

D_MODEL = 1024
PLE_DIM = 256
D_CONV = 512
N_HEADS = 4
HEAD_DIM = 128
D_DN = N_HEADS * HEAD_DIM
CHUNK = 64
D_FF = 4 * D_MODEL
EPS = 1e-6

LANES = 128
VMEM_LIMIT = 56 * 1024 * 1024

TS_FRONT = 512
TC_DELTA = 256
DELTA_CHUNKS_PER_ITER = 2
TM_OUT = 512
FF_CHUNK = 1024

W_PA = 0
W_QKV = W_PA + 3 * D_CONV
W_Z = W_QKV + 3 * D_DN
W_SM = W_Z + D_DN
W_GATES = W_SM + 4 * N_HEADS
W_END = W_GATES + 2 * D_MODEL
W_GROUPS = ((W_PA, W_QKV), (W_QKV, W_Z), (W_Z, W_SM), (W_SM, W_SM + LANES), (W_GATES, W_END))

CS_BETA, CS_G, CS_EG, CS_EKG, CS_EGT = 0, 8, 16, 24, 32


def _resident(shape):
    nd = len(shape)
    return pl.BlockSpec(shape, lambda *_: (0,) * nd, pipeline_mode=pl.Buffered(1))


def _rms(x, g):
    return x * lax.rsqrt(jnp.mean(x * x, axis=-1, keepdims=True) + EPS) * g


def _dot(a, b):
    return jnp.dot(a, b, preferred_element_type=F32)


def _silu(x):
    return x * jax.nn.sigmoid(x)


HALO = 8


def _conv3(ext, w, ts):
    n = ext.shape[0]
    up = pltpu.roll(ext, 1, 0)[HALO:HALO + ts]
    dn = pltpu.roll(ext, n - 1, 0)[HALO:HALO + ts]
    return up * w[0:1, :] + ext[HALO:HALO + ts] * w[1:2, :] + dn * w[2:3, :]


def _chunk_prefix_sums(g):
    ts = g.shape[0]
    hi = g.astype(BF16).astype(F32)
    r1 = g - hi
    mid = r1.astype(BF16).astype(F32)
    lo = (r1 - mid).astype(BF16).astype(F32)
    packed = (hi + pltpu.roll(mid, 16, 1) + pltpu.roll(lo, 32, 1)).astype(BF16)
    r = lax.broadcasted_iota(jnp.int32, (CHUNK, CHUNK), 0)
    c = lax.broadcasted_iota(jnp.int32, (CHUNK, CHUNK), 1)
    tri = jnp.where(c <= r, 1.0, 0.0).astype(BF16)
    res = jnp.concatenate([_dot(tri, packed[n * CHUNK:(n + 1) * CHUNK]) for n in range(ts // CHUNK)], axis=0)
    return res + pltpu.roll(res, LANES - 16, 1) + pltpu.roll(res, LANES - 32, 1)


def _front_kernel(x_ref, xp_ref, xn_ref, gmix_ref, wt_ref,
                  conv_a_ref, conv_qkv_ref, alog_ref, dtb_ref, *rest):
    nw = (len(rest) - 8 - len(W_GROUPS)) // 2
    wsrc_refs, (ya_ref, q_ref, k_ref, v_ref, cs_ref, gt_ref, z_ref, gates_ref), wdst_refs = (
        rest[:nw], rest[nw:nw + 8], rest[nw + 8:2 * nw + 8])
    wpa_ref, wqkv_ref, wz_ref, wsm_ref, wg_ref = w_refs = rest[2 * nw + 8:]
    j = pl.program_id(1)
    nj = pl.num_programs(1)
    ts = TS_FRONT

    @pl.when((pl.program_id(0) == 0) & (j == 0))
    def _():
        for (lo, hi), dst in zip(W_GROUPS, w_refs):
            dst[...] = wt_ref[lo:hi, :].T

    has_prev = (j > 0).astype(F32)
    has_next = (j < nj - 1).astype(F32)
    inner = slice(HALO, HALO + ts)

    gm = gmix_ref[...]
    n_cur = _rms(x_ref[0], gm)
    n_ext = jnp.concatenate([_rms(xp_ref[0], gm) * has_prev, n_cur, _rms(xn_ref[0], gm) * has_next],
                            axis=0).astype(BF16)
    n_b = n_cur.astype(BF16)

    sm = _dot(n_b, wsm_ref[...])
    lane = lax.broadcasted_iota(jnp.int32, sm.shape, 1)
    beta = jax.nn.sigmoid(sm)
    xs = sm + dtb_ref[...]
    softplus = jnp.maximum(xs, 0.0) + jnp.log1p(jnp.exp(-jnp.abs(xs)))
    g = jnp.where((lane >= 8) & (lane < 16), -jnp.exp(alog_ref[...]) * softplus, 0.0)

    pa = _dot(n_ext, wpa_ref[...])
    u = pa[:, :D_CONV] * pa[:, 2 * D_CONV:]
    ya = pa[inner, D_CONV:2 * D_CONV] * _conv3(u, conv_a_ref[...], ts)
    ya_ref[0] = ya.astype(ya_ref.dtype)

    pre = _chunk_prefix_sums(g)
    pre3 = pre.reshape(ts // CHUNK, CHUNK, LANES)
    tot = jnp.broadcast_to(pre3[:, CHUNK - 1:CHUNK, :], pre3.shape).reshape(ts, LANES)
    gc = jnp.where(lane < 12, pre, tot - pre + g)
    eg = jnp.exp(gc)
    ekg = jnp.exp(tot - gc)
    egt = jnp.exp(tot)

    g8 = pltpu.roll(gc, LANES - 8, 1)
    cs = jnp.where(lane < 8, beta, 0.0)
    cs = jnp.where((lane >= 8) & (lane < 16), gc, cs)
    cs = jnp.where((lane >= 16) & (lane < 24), pltpu.roll(eg, 8, 1), cs)
    cs = jnp.where((lane >= 24) & (lane < 32), pltpu.roll(ekg, 16, 1), cs)
    cs = jnp.where((lane >= 32) & (lane < 40), pltpu.roll(egt, 24, 1), cs)
    cs_ref[0] = cs

    g8t = g8.T
    for n in range(ts // CHUNK):
        gt_ref[0, n] = g8t[0:8, n * CHUNK:(n + 1) * CHUNK]

    for gi, (ref, scale) in enumerate(((q_ref, HEAD_DIM ** -0.5), (k_ref, None), (v_ref, None))):
        sl = slice(gi * D_DN, (gi + 1) * D_DN)
        c = _silu(_conv3(_dot(n_ext, wqkv_ref[:, sl]), conv_qkv_ref[:, sl], ts))
        if ref is v_ref:
            v_ref[0] = c
            continue
        for h in range(N_HEADS):
            hs = slice(h * HEAD_DIM, (h + 1) * HEAD_DIM)
            ch = c[:, hs]
            cn = ch * lax.rsqrt(jnp.sum(ch * ch, axis=-1, keepdims=True) + EPS)
            ref[0, :, hs] = cn if scale is None else cn * scale

    z_ref[0] = _dot(n_b, wz_ref[...]).astype(z_ref.dtype)
    gates_ref[0] = _dot(n_b, wg_ref[...]).astype(gates_ref.dtype)

    for src, dst in zip(wsrc_refs, wdst_refs):
        dst[...] = src[...].astype(dst.dtype)


def _front(x, g_mix, wt, conv_a, conv_qkv, alog, dtb, cast_weights):
    b, s, d = x.shape
    ts = TS_FRONT
    nj = s // ts
    r8 = ts // HALO
    n8 = s // HALO

    def cur(w):
        return pl.BlockSpec((1, ts, w), lambda i, j: (i, j, 0))

    prev = pl.BlockSpec((1, HALO, d), lambda i, j: (i, jnp.maximum(j * r8 - 1, 0), 0))
    nxt = pl.BlockSpec((1, HALO, d), lambda i, j: (i, jnp.minimum((j + 1) * r8, n8 - 1), 0))
    consts = (g_mix, wt, conv_a, conv_qkv, alog, dtb)
    slab = lambda a: pl.BlockSpec((a.shape[0] // (b * nj), a.shape[1]), lambda i, j: (i * nj + j, 0))
    return pl.pallas_call(
        _front_kernel,
        grid=(b, nj),
        in_specs=[cur(d), prev, nxt] + [_resident(c.shape) for c in consts] + [slab(a) for a in cast_weights],
        out_specs=[cur(D_CONV), cur(D_DN), cur(D_DN), cur(D_DN), cur(LANES),
                   pl.BlockSpec((1, ts // CHUNK, 8, CHUNK), lambda i, j: (i, j, 0, 0)),
                   cur(D_DN), cur(2 * D_MODEL)] + [slab(a) for a in cast_weights],
        out_shape=[jax.ShapeDtypeStruct((b, s, D_CONV), BF16),
                   jax.ShapeDtypeStruct((b, s, D_DN), F32),
                   jax.ShapeDtypeStruct((b, s, D_DN), F32),
                   jax.ShapeDtypeStruct((b, s, D_DN), F32),
                   jax.ShapeDtypeStruct((b, s, LANES), F32),
                   jax.ShapeDtypeStruct((b, s // CHUNK, 8, CHUNK), F32),
                   jax.ShapeDtypeStruct((b, s, D_DN), BF16),
                   jax.ShapeDtypeStruct((b, s, 2 * D_MODEL), BF16)]
        + [jax.ShapeDtypeStruct(a.shape, BF16) for a in cast_weights],
        scratch_shapes=[pltpu.VMEM((d, hi - lo), BF16) for lo, hi in W_GROUPS],
        compiler_params=pltpu.CompilerParams(dimension_semantics=("arbitrary", "arbitrary"),
                                             vmem_limit_bytes=VMEM_LIMIT),
        name="front",
    )(x, x, x, *consts, *cast_weights)


PAIR = 2 * HEAD_DIM


def _block_diag(x, half):
    lane = lax.broadcasted_iota(jnp.int32, x.shape, 1)
    zero = jnp.zeros_like(x)
    return jnp.concatenate([jnp.where(lane < half, x, zero), jnp.where(lane >= half, x, zero)], axis=0)


def _delta_prep(items):
    r = lax.broadcasted_iota(jnp.int32, (CHUNK, 2 * CHUNK), 0)
    lane = lax.broadcasted_iota(jnp.int32, (CHUNK, 2 * CHUNK), 1)
    c = jnp.where(lane < CHUNK, lane, lane - CHUNK)
    first = lane < CHUNK
    eye = jnp.where(r == c, 1.0, 0.0)

    def col256(cs, base, c0):
        return jnp.concatenate([jnp.broadcast_to(cs[:, base + c0:base + c0 + 1], (CHUNK, HEAD_DIM)),
                                jnp.broadcast_to(cs[:, base + c0 + 1:base + c0 + 2], (CHUNK, HEAD_DIM))], axis=1)

    for it in items:
        cs, c0 = it["cs"], it["c0"]
        it["beta"] = col256(cs, CS_BETA, c0)
        it["eg"] = col256(cs, CS_EG, c0)
        it["kb"] = it["k"] * it["beta"]
        lhs = jnp.concatenate([it["kb"], it["q"]], axis=0).astype(BF16)
        it["kq"] = lax.dot_general(lhs, _block_diag(it["k"].astype(BF16), HEAD_DIM),
                                   (((1,), (1,)), ((), ())), preferred_element_type=F32)
    yield

    for it in items:
        cs, c0 = it["cs"], it["c0"]
        incl = (r >= c) if it["lower"] else (r <= c)
        strict = (r > c) if it["lower"] else (r < c)
        gcol = jnp.where(first, cs[:, CS_G + c0:CS_G + c0 + 1], cs[:, CS_G + c0 + 1:CS_G + c0 + 2])
        decay = jnp.where(incl, jnp.exp(jnp.where(incl, gcol - it["grow"], 0.0)), 0.0)
        a = jnp.where(strict, it["kq"][:CHUNK] * decay, 0.0)
        it["intra"] = (it["kq"][CHUNK:] * decay).astype(BF16)
        it["p"] = eye - a
        it["a"] = a.astype(BF16)

    for it in items:
        it["y"] = _dot(it["a"], _block_diag(it["a"], CHUNK)).astype(BF16)
    yield
    steps = 2
    while 2 * steps < CHUNK:
        steps *= 2
        for it in items:
            res = _dot(jnp.concatenate([it["p"].astype(BF16), it["y"]], axis=0), _block_diag(it["y"], CHUNK))
            it["p"] = it["p"] + res[:CHUNK]
            it["y"] = res[CHUNK:].astype(BF16)
        yield
    for it in items:
        it["p"] = it["p"] + _dot(it["p"].astype(BF16), _block_diag(it["y"], CHUNK))
    yield

    for it in items:
        vb = (it["v"] * it["beta"]).astype(BF16)
        kbg = (it["kb"] * it["eg"]).astype(BF16)
        rhs = jnp.concatenate([_block_diag(vb, HEAD_DIM), _block_diag(kbg, HEAD_DIM)], axis=1)
        uw = _dot(it["p"].astype(BF16), rhs)
        it["u"] = uw[:, :PAIR]
        it["w"] = uw[:, PAIR:]
        it["qg"] = it["q"] * it["eg"]
        kg = (it["k"] * col256(it["cs"], CS_EKG, it["c0"])).astype(BF16)
        it["kgs"] = jnp.concatenate([kg[:, :HEAD_DIM], kg[:, HEAD_DIM:]], axis=0)
    yield


def _delta_scan(rounds, s_ref):
    for items in rounds:
        for it in items:
            it["state"] = s_ref[it["sidx"]]
            sb = it["state"].astype(BF16)
            ws = []
            for h in range(2):
                sl = slice(h * HEAD_DIM, (h + 1) * HEAD_DIM)
                lhs = jnp.concatenate([it["w"][:, sl], it["qg"][:, sl]], axis=0).astype(BF16)
                ws.append(_dot(lhs, sb[:, sl]))
            it["ws_top"] = jnp.concatenate([ws[0][:CHUNK], ws[1][:CHUNK]], axis=1)
            it["ws_bot"] = jnp.concatenate([ws[0][CHUNK:], ws[1][CHUNK:]], axis=1)
        yield

        for it in items:
            cs, c0 = it["cs"], it["c0"]
            v_new = _block_diag((it["u"] - it["ws_top"]).astype(BF16), HEAD_DIM)
            o_ref, bi, rows, sl = it["dst"]
            o_ref[bi, rows, sl] = it["ws_bot"] + _dot(it["intra"], v_new)
            upd = lax.dot_general(it["kgs"], v_new, (((0,), (0,)), ((), ())), preferred_element_type=F32)
            egt = jnp.concatenate([jnp.broadcast_to(cs[0:1, CS_EGT + c0:CS_EGT + c0 + 1], (1, HEAD_DIM)),
                                   jnp.broadcast_to(cs[0:1, CS_EGT + c0 + 1:CS_EGT + c0 + 2], (1, HEAD_DIM))], axis=1)
            s_ref[it["sidx"]] = it["state"] * egt + upd
        yield


def _interleave(*stage_gens):
    live = list(stage_gens)
    while live:
        for g in list(live):
            if next(g, StopIteration) is StopIteration:
                live.remove(g)


def _delta_kernel(qf_ref, kf_ref, vf_ref, csf_ref, gtf_ref, qb_ref, kb_ref, vb_ref, csb_ref, gtb_ref,
                  of_ref, ob_ref, s_ref):
    @pl.when(pl.program_id(0) == 0)
    def _():
        s_ref[...] = jnp.zeros_like(s_ref)

    nbatch = qf_ref.shape[0]
    nc = TC_DELTA // CHUNK
    npairs = N_HEADS // 2

    def chunk_items(jf):
        items = []
        for bi in range(nbatch):
            for lower, j, (q_ref, k_ref, v_ref, cs_ref, gt_ref, o_ref) in (
                    (True, jf, (qf_ref, kf_ref, vf_ref, csf_ref, gtf_ref, of_ref)),
                    (False, nc - 1 - jf, (qb_ref, kb_ref, vb_ref, csb_ref, gtb_ref, ob_ref))):
                d = 0 if lower else 1
                rows = slice(j * CHUNK, (j + 1) * CHUNK)
                cs = cs_ref[bi, rows, :]
                gt = gt_ref[bi, j]
                for hp in range(npairs):
                    sl = slice(hp * PAIR, (hp + 1) * PAIR)
                    items.append(dict(
                        q=q_ref[bi, rows, sl], k=k_ref[bi, rows, sl], v=v_ref[bi, rows, sl], cs=cs,
                        grow=gt[d * npairs + hp:d * npairs + hp + 1, :], c0=d * N_HEADS + 2 * hp,
                        sidx=(bi * 2 + d) * npairs + hp, lower=lower, dst=(o_ref, bi, rows, sl)))
        return items

    groups = [[chunk_items(g * DELTA_CHUNKS_PER_ITER + cc) for cc in range(DELTA_CHUNKS_PER_ITER)]
              for g in range(nc // DELTA_CHUNKS_PER_ITER)]
    pending = None
    for rounds in groups:
        prep = _delta_prep([it for items in rounds for it in items])
        if pending is None:
            _interleave(prep)
        else:
            _interleave(prep, _delta_scan(pending, s_ref))
        pending = rounds
    _interleave(_delta_scan(pending, s_ref))


def _delta(q, k, v, cs, gt):
    b, s, _ = q.shape
    tc = TC_DELTA
    nb = s // tc
    ncb = tc // CHUNK
    fwd = lambda w: pl.BlockSpec((b, tc, w), lambda j: (0, j, 0))
    bwd = lambda w: pl.BlockSpec((b, tc, w), lambda j: (0, nb - 1 - j, 0))
    gt_f = pl.BlockSpec((b, ncb, N_HEADS, 2 * CHUNK), lambda j: (0, j, 0, 0))
    gt_b = pl.BlockSpec((b, ncb, N_HEADS, 2 * CHUNK), lambda j: (0, nb - 1 - j, 0, 0))
    return pl.pallas_call(
        _delta_kernel,
        grid=(nb,),
        in_specs=[fwd(D_DN), fwd(D_DN), fwd(D_DN), fwd(LANES), gt_f,
                  bwd(D_DN), bwd(D_DN), bwd(D_DN), bwd(LANES), gt_b],
        out_specs=[fwd(D_DN), bwd(D_DN)],
        out_shape=[jax.ShapeDtypeStruct((b, s, D_DN), F32), jax.ShapeDtypeStruct((b, s, D_DN), F32)],
        scratch_shapes=[pltpu.VMEM((b * N_HEADS, HEAD_DIM, PAIR), F32)],
        compiler_params=pltpu.CompilerParams(dimension_semantics=("arbitrary",),
                                             vmem_limit_bytes=VMEM_LIMIT),
        name="delta",
    )(q, k, v, cs, gt, q, k, v, cs, gt)


def _out_kernel(x_ref, p_ref, of_ref, ob_ref, z_ref, ya_ref, gates_ref,
                gdn_ref, wa_ref, wdn_ref, wo_ref, gffn_ref, w1_ref, w2_ref,
                gple_ref, wpg_ref, wpp_ref, gfin_ref, out_ref):
    o = of_ref[...] + ob_ref[...]
    z = z_ref[...].astype(F32)
    gdn = gdn_ref[...]
    parts = []
    for h in range(N_HEADS):
        sl = slice(h * HEAD_DIM, (h + 1) * HEAD_DIM)
        parts.append(_rms(o[:, sl], gdn) * _silu(z[:, sl]))
    y_dn = jnp.concatenate(parts, axis=1).astype(BF16)
    gates = gates_ref[...].astype(F32)
    merged = (jax.nn.sigmoid(gates[:, :D_MODEL]) * _dot(ya_ref[...], wa_ref[...])
              + jax.nn.sigmoid(gates[:, D_MODEL:]) * _dot(y_dn, wdn_ref[...]))
    h1 = x_ref[...] + _dot(merged.astype(BF16), wo_ref[...])

    n2 = _rms(h1, gffn_ref[...]).astype(BF16)
    h2 = h1
    for c in range(D_FF // FF_CHUNK):
        sl = slice(c * FF_CHUNK, (c + 1) * FF_CHUNK)
        a = jnp.maximum(_dot(n2, w1_ref[:, sl]), 0.0)
        h2 = h2 + _dot((a * a).astype(BF16), w2_ref[sl, :])

    n3 = _rms(h2, gple_ref[...]).astype(BF16)
    gate_p = jax.nn.sigmoid(_dot(n3, wpg_ref[...]))
    h3 = h2 + gate_p * _dot(p_ref[...].astype(BF16), wpp_ref[...])
    out_ref[...] = _rms(h3, gfin_ref[...])


def _out(x2, p2, o_f, o_b, z, ya, gates, gdn, wa, wdn, wo, gffn, w1, w2, gple, wpg, wpp, gfin):
    t = x2.shape[0]
    tm = TM_OUT
    row = lambda w: pl.BlockSpec((tm, w), lambda i: (i, 0))
    weights = (gdn, wa, wdn, wo, gffn, w1, w2, gple, wpg, wpp, gfin)
    return pl.pallas_call(
        _out_kernel,
        grid=(t // tm,),
        in_specs=[row(D_MODEL), row(PLE_DIM), row(D_DN), row(D_DN), row(D_DN), row(D_CONV),
                  row(2 * D_MODEL)] + [_resident(w.shape) for w in weights],
        out_specs=row(D_MODEL),
        out_shape=jax.ShapeDtypeStruct((t, D_MODEL), F32),
        compiler_params=pltpu.CompilerParams(dimension_semantics=("arbitrary",),
                                             vmem_limit_bytes=VMEM_LIMIT),
        name="out",
    )(x2, p2, o_f, o_b, z, ya, gates, *weights)


def kernel(x, p, g_mix, w_in, conv_a, conv_qkv, a_log_f, a_log_b, dt_bias_f, dt_bias_b, g_dn_norm,
           w_a_out, w_dn_out, w_o, g_ffn, w_ff1, w_ff2, g_ple, w_ple_gate, w_ple_proj, g_final):
    b, s, d = x.shape
    depth = p.shape[0]
    assert depth == 1 and d == D_MODEL and s % TC_DELTA == 0 and s % TS_FRONT == 0
    t = b * s
    i = 0

    wt = jnp.transpose(w_in[i]).astype(BF16)

    zeros8 = jnp.zeros((2 * N_HEADS,), F32)
    tail = jnp.zeros((LANES - 4 * N_HEADS,), F32)
    alog = jnp.concatenate([zeros8, a_log_f[i], a_log_b[i], tail])[None, :]
    dtb = jnp.concatenate([zeros8, dt_bias_f[i], dt_bias_b[i], tail])[None, :]
    out_weights = (w_a_out[i], w_dn_out[i], w_o[i], w_ff1[i], w_ff2[i], w_ple_gate[i])
    ya, q, k, v, cs, gt, z, gates, *out_weights = _front(x, g_mix[i][None, :], wt,
                                                         conv_a[i], conv_qkv[i], alog, dtb, out_weights)
    wa, wdn, wo, w1, w2, wpg = out_weights
    wpp = w_ple_proj[i].astype(BF16)

    gt = gt.reshape(b, s // CHUNK, N_HEADS, 2 * CHUNK)
    o_f, o_b = _delta(q, k, v, cs, gt)

    out = _out(x.reshape(t, d), p[i].reshape(t, PLE_DIM), o_f.reshape(t, D_DN), o_b.reshape(t, D_DN),
               z.reshape(t, D_DN), ya.reshape(t, D_CONV), gates.reshape(t, 2 * D_MODEL),
               g_dn_norm[i][None, :], wa, wdn, wo, g_ffn[i][None, :], w1, w2,
               g_ple[i][None, :], wpg, wpp, g_final[None, :])
    return out.reshape(b, s, d)
```

```python
import functools

import jax
import jax.numpy as jnp
from jax import lax
from jax.experimental import pallas as pl
from jax.experimental.pallas import tpu as pltpu

F32 = jnp.float32
BF16 = jnp.bfloat16

D_MODEL = 1024
PLE_DIM = 256
D_CONV = 512
N_HEADS = 4
HEAD_DIM = 128
D_DN = N_HEADS * HEAD_DIM
CHUNK = 64
D_FF = 4 * D_MODEL
EPS = 1e-6

LANES = 128
VMEM_LIMIT = 56 * 1024 * 1024

TS_FRONT = 512
TC_DELTA = 512
DELTA_CHUNKS_PER_ITER = 2
TM_OUT = 512
FF_CHUNK = 1024

W_PA = 0
W_QKV = W_PA + 3 * D_CONV
W_Z = W_QKV + 3 * D_DN
W_SM = W_Z + D_DN
W_GATES = W_SM + 4 * N_HEADS
W_END = W_GATES + 2 * D_MODEL
W_GROUPS = ((W_PA, W_QKV), (W_QKV, W_Z), (W_Z, W_SM), (W_SM, W_SM + LANES), (W_GATES, W_END))

CS_BETA, CS_G, CS_EG, CS_EKG, CS_EGT = 0, 8, 16, 24, 32


def _resident(shape):
    nd = len(shape)
    return pl.BlockSpec(shape, lambda *_: (0,) * nd, pipeline_mode=pl.Buffered(1))


def _rms(x, g):
    return x * lax.rsqrt(jnp.mean(x * x, axis=-1, keepdims=True) + EPS) * g


def _dot(a, b):
    return jnp.dot(a, b, preferred_element_type=F32)


def _silu(x):
    return x * jax.nn.sigmoid(x)


HALO = 8


def _conv3(ext, w, ts):
    n = ext.shape[0]
    up = pltpu.roll(ext, 1, 0)[HALO:HALO + ts]
    dn = pltpu.roll(ext, n - 1, 0)[HALO:HALO + ts]
    return up * w[0:1, :] + ext[HALO:HALO + ts] * w[1:2, :] + dn * w[2:3, :]


def _chunk_prefix_sums(g):
    ts = g.shape[0]
    hi = g.astype(BF16).astype(F32)
    r1 = g - hi
    mid = r1.astype(BF16).astype(F32)
    lo = (r1 - mid).astype(BF16).astype(F32)
    packed = (hi + pltpu.roll(mid, 16, 1) + pltpu.roll(lo, 32, 1)).astype(BF16)
    r = lax.broadcasted_iota(jnp.int32, (CHUNK, CHUNK), 0)
    c = lax.broadcasted_iota(jnp.int32, (CHUNK, CHUNK), 1)
    tri = jnp.where(c <= r, 1.0, 0.0).astype(BF16)
    res = jnp.concatenate([_dot(tri, packed[n * CHUNK:(n + 1) * CHUNK]) for n in range(ts // CHUNK)], axis=0)
    return res + pltpu.roll(res, LANES - 16, 1) + pltpu.roll(res, LANES - 32, 1)


def _front_kernel(x_ref, xp_ref, xn_ref, gmix_ref, wt_ref,
                  conv_a_ref, conv_qkv_ref, alog_ref, dtb_ref, *rest):
    nw = (len(rest) - 8 - len(W_GROUPS)) // 2
    wsrc_refs, (ya_ref, q_ref, k_ref, v_ref, cs_ref, gt_ref, z_ref, gates_ref), wdst_refs = (
        rest[:nw], rest[nw:nw + 8], rest[nw + 8:2 * nw + 8])
    wpa_ref, wqkv_ref, wz_ref, wsm_ref, wg_ref = w_refs = rest[2 * nw + 8:]
    j = pl.program_id(1)
    nj = pl.num_programs(1)
    ts = TS_FRONT

    @pl.when((pl.program_id(0) == 0) & (j == 0))
    def _():
        for (lo, hi), dst in zip(W_GROUPS, w_refs):
            dst[...] = wt_ref[lo:hi, :].T

    has_prev = (j > 0).astype(F32)
    has_next = (j < nj - 1).astype(F32)
    inner = slice(HALO, HALO + ts)

    gm = gmix_ref[...]
    n_cur = _rms(x_ref[0], gm)
    n_ext = jnp.concatenate([_rms(xp_ref[0], gm) * has_prev, n_cur, _rms(xn_ref[0], gm) * has_next],
                            axis=0).astype(BF16)
    n_b = n_cur.astype(BF16)

    sm = _dot(n_b, wsm_ref[...])
    lane = lax.broadcasted_iota(jnp.int32, sm.shape, 1)
    beta = jax.nn.sigmoid(sm)
    xs = sm + dtb_ref[...]
    softplus = jnp.maximum(xs, 0.0) + jnp.log1p(jnp.exp(-jnp.abs(xs)))
    g = jnp.where((lane >= 8) & (lane < 16), -jnp.exp(alog_ref[...]) * softplus, 0.0)

    pa = _dot(n_ext, wpa_ref[...])
    u = pa[:, :D_CONV] * pa[:, 2 * D_CONV:]
    ya = pa[inner, D_CONV:2 * D_CONV] * _conv3(u, conv_a_ref[...], ts)
    ya_ref[0] = ya.astype(ya_ref.dtype)

    pre = _chunk_prefix_sums(g)
    pre3 = pre.reshape(ts // CHUNK, CHUNK, LANES)
    tot = jnp.broadcast_to(pre3[:, CHUNK - 1:CHUNK, :], pre3.shape).reshape(ts, LANES)
    gc = jnp.where(lane < 12, pre, tot - pre + g)
    eg = jnp.exp(gc)
    ekg = jnp.exp(tot - gc)
    egt = jnp.exp(tot)

    g8 = pltpu.roll(gc, LANES - 8, 1)
    cs = jnp.where(lane < 8, beta, 0.0)
    cs = jnp.where((lane >= 8) & (lane < 16), gc, cs)
    cs = jnp.where((lane >= 16) & (lane < 24), pltpu.roll(eg, 8, 1), cs)
    cs = jnp.where((lane >= 24) & (lane < 32), pltpu.roll(ekg, 16, 1), cs)
    cs = jnp.where((lane >= 32) & (lane < 40), pltpu.roll(egt, 24, 1), cs)
    cs_ref[0] = cs

    g8t = g8.T
    for n in range(ts // CHUNK):
        gt_ref[0, n] = g8t[0:8, n * CHUNK:(n + 1) * CHUNK]

    for gi, (ref, scale) in enumerate(((q_ref, HEAD_DIM ** -0.5), (k_ref, None), (v_ref, None))):
        sl = slice(gi * D_DN, (gi + 1) * D_DN)
        c = _silu(_conv3(_dot(n_ext, wqkv_ref[:, sl]), conv_qkv_ref[:, sl], ts))
        if ref is v_ref:
            v_ref[0] = c
            continue
        for h in range(N_HEADS):
            hs = slice(h * HEAD_DIM, (h + 1) * HEAD_DIM)
            ch = c[:, hs]
            cn = ch * lax.rsqrt(jnp.sum(ch * ch, axis=-1, keepdims=True) + EPS)
            ref[0, :, hs] = cn if scale is None else cn * scale

    z_ref[0] = _dot(n_b, wz_ref[...]).astype(z_ref.dtype)
    gates_ref[0] = _dot(n_b, wg_ref[...]).astype(gates_ref.dtype)

    for src, dst in zip(wsrc_refs, wdst_refs):
        dst[...] = src[...].astype(dst.dtype)


def _front(x, g_mix, wt, conv_a, conv_qkv, alog, dtb, cast_weights):
    b, s, d = x.shape
    ts = TS_FRONT
    nj = s // ts
    r8 = ts // HALO
    n8 = s // HALO

    def cur(w):
        return pl.BlockSpec((1, ts, w), lambda i, j: (i, j, 0))

    prev = pl.BlockSpec((1, HALO, d), lambda i, j: (i, jnp.maximum(j * r8 - 1, 0), 0))
    nxt = pl.BlockSpec((1, HALO, d), lambda i, j: (i, jnp.minimum((j + 1) * r8, n8 - 1), 0))
    consts = (g_mix, wt, conv_a, conv_qkv, alog, dtb)
    slab = lambda a: pl.BlockSpec((a.shape[0] // (b * nj), a.shape[1]), lambda i, j: (i * nj + j, 0))
    return pl.pallas_call(
        _front_kernel,
        grid=(b, nj),
        in_specs=[cur(d), prev, nxt] + [_resident(c.shape) for c in consts] + [slab(a) for a in cast_weights],
        out_specs=[cur(D_CONV), cur(D_DN), cur(D_DN), cur(D_DN), cur(LANES),
                   pl.BlockSpec((1, ts // CHUNK, 8, CHUNK), lambda i, j: (i, j, 0, 0)),
                   cur(D_DN), cur(2 * D_MODEL)] + [slab(a) for a in cast_weights],
        out_shape=[jax.ShapeDtypeStruct((b, s, D_CONV), BF16),
                   jax.ShapeDtypeStruct((b, s, D_DN), F32),
                   jax.ShapeDtypeStruct((b, s, D_DN), F32),
                   jax.ShapeDtypeStruct((b, s, D_DN), F32),
                   jax.ShapeDtypeStruct((b, s, LANES), F32),
                   jax.ShapeDtypeStruct((b, s // CHUNK, 8, CHUNK), F32),
                   jax.ShapeDtypeStruct((b, s, D_DN), BF16),
                   jax.ShapeDtypeStruct((b, s, 2 * D_MODEL), BF16)]
        + [jax.ShapeDtypeStruct(a.shape, BF16) for a in cast_weights],
        scratch_shapes=[pltpu.VMEM((d, hi - lo), BF16) for lo, hi in W_GROUPS],
        compiler_params=pltpu.CompilerParams(dimension_semantics=("arbitrary", "arbitrary"),
                                             vmem_limit_bytes=VMEM_LIMIT),
        name="front",
    )(x, x, x, *consts, *cast_weights)


PAIR = 2 * HEAD_DIM


def _block_diag(x, half):
    lane = lax.broadcasted_iota(jnp.int32, x.shape, 1)
    zero = jnp.zeros_like(x)
    return jnp.concatenate([jnp.where(lane < half, x, zero), jnp.where(lane >= half, x, zero)], axis=0)


def _delta_prep(items):
    r = lax.broadcasted_iota(jnp.int32, (CHUNK, 2 * CHUNK), 0)
    lane = lax.broadcasted_iota(jnp.int32, (CHUNK, 2 * CHUNK), 1)
    c = jnp.where(lane < CHUNK, lane, lane - CHUNK)
    first = lane < CHUNK
    eye = jnp.where(r == c, 1.0, 0.0)

    def col256(cs, base, c0):
        return jnp.concatenate([jnp.broadcast_to(cs[:, base + c0:base + c0 + 1], (CHUNK, HEAD_DIM)),
                                jnp.broadcast_to(cs[:, base + c0 + 1:base + c0 + 2], (CHUNK, HEAD_DIM))], axis=1)

    for it in items:
        cs, c0 = it["cs"], it["c0"]
        it["beta"] = col256(cs, CS_BETA, c0)
        it["eg"] = col256(cs, CS_EG, c0)
        it["kb"] = it["k"] * it["beta"]
        lhs = jnp.concatenate([it["kb"], it["q"]], axis=0).astype(BF16)
        it["kq"] = lax.dot_general(lhs, _block_diag(it["k"].astype(BF16), HEAD_DIM),
                                   (((1,), (1,)), ((), ())), preferred_element_type=F32)
    yield

    for it in items:
        cs, c0 = it["cs"], it["c0"]
        incl = (r >= c) if it["lower"] else (r <= c)
        strict = (r > c) if it["lower"] else (r < c)
        gcol = jnp.where(first, cs[:, CS_G + c0:CS_G + c0 + 1], cs[:, CS_G + c0 + 1:CS_G + c0 + 2])
        decay = jnp.where(incl, jnp.exp(jnp.where(incl, gcol - it["grow"], 0.0)), 0.0)
        a = jnp.where(strict, it["kq"][:CHUNK] * decay, 0.0)
        it["intra"] = (it["kq"][CHUNK:] * decay).astype(BF16)
        it["p"] = eye - a
        it["a"] = a.astype(BF16)

    for it in items:
        it["y"] = _dot(it["a"], _block_diag(it["a"], CHUNK)).astype(BF16)
    yield
    steps = 2
    while 2 * steps < CHUNK:
        steps *= 2
        for it in items:
            res = _dot(jnp.concatenate([it["p"].astype(BF16), it["y"]], axis=0), _block_diag(it["y"], CHUNK))
            it["p"] = it["p"] + res[:CHUNK]
            it["y"] = res[CHUNK:].astype(BF16)
        yield
    for it in items:
        it["p"] = it["p"] + _dot(it["p"].astype(BF16), _block_diag(it["y"], CHUNK))
    yield

    for it in items:
        vb = (it["v"] * it["beta"]).astype(BF16)
        kbg = (it["kb"] * it["eg"]).astype(BF16)
        rhs = jnp.concatenate([_block_diag(vb, HEAD_DIM), _block_diag(kbg, HEAD_DIM)], axis=1)
        uw = _dot(it["p"].astype(BF16), rhs)
        it["u"] = uw[:, :PAIR]
        it["w"] = uw[:, PAIR:]
        it["qg"] = it["q"] * it["eg"]
        kg = (it["k"] * col256(it["cs"], CS_EKG, it["c0"])).astype(BF16)
        it["kgs"] = jnp.concatenate([kg[:, :HEAD_DIM], kg[:, HEAD_DIM:]], axis=0)
    yield


def _delta_scan(rounds, s_ref):
    for items in rounds:
        for it in items:
            it["state"] = s_ref[it["sidx"]]
            sb = it["state"].astype(BF16)
            ws = []
            for h in range(2):
                sl = slice(h * HEAD_DIM, (h + 1) * HEAD_DIM)
                lhs = jnp.concatenate([it["w"][:, sl], it["qg"][:, sl]], axis=0).astype(BF16)
                ws.append(_dot(lhs, sb[:, sl]))
            it["ws_top"] = jnp.concatenate([ws[0][:CHUNK], ws[1][:CHUNK]], axis=1)
            it["ws_bot"] = jnp.concatenate([ws[0][CHUNK:], ws[1][CHUNK:]], axis=1)
        yield

        for it in items:
            cs, c0 = it["cs"], it["c0"]
            v_new = _block_diag((it["u"] - it["ws_top"]).astype(BF16), HEAD_DIM)
            o_ref, bi, rows, sl = it["dst"]
            o_ref[bi, rows, sl] = it["ws_bot"] + _dot(it["intra"], v_new)
            upd = lax.dot_general(it["kgs"], v_new, (((0,), (0,)), ((), ())), preferred_element_type=F32)
            egt = jnp.concatenate([jnp.broadcast_to(cs[0:1, CS_EGT + c0:CS_EGT + c0 + 1], (1, HEAD_DIM)),
                                   jnp.broadcast_to(cs[0:1, CS_EGT + c0 + 1:CS_EGT + c0 + 2], (1, HEAD_DIM))], axis=1)
            s_ref[it["sidx"]] = it["state"] * egt + upd
        yield


def _interleave(*stage_gens):
    live = list(stage_gens)
    while live:
        for g in list(live):
            if next(g, StopIteration) is StopIteration:
                live.remove(g)


def _delta_kernel(qf_ref, kf_ref, vf_ref, csf_ref, gtf_ref, qb_ref, kb_ref, vb_ref, csb_ref, gtb_ref,
                  of_ref, ob_ref, s_ref):
    @pl.when(pl.program_id(0) == 0)
    def _():
        s_ref[...] = jnp.zeros_like(s_ref)

    nbatch = qf_ref.shape[0]
    nc = TC_DELTA // CHUNK
    npairs = N_HEADS // 2

    def chunk_items(jf):
        items = []
        for bi in range(nbatch):
            for lower, j, (q_ref, k_ref, v_ref, cs_ref, gt_ref, o_ref) in (
                    (True, jf, (qf_ref, kf_ref, vf_ref, csf_ref, gtf_ref, of_ref)),
                    (False, nc - 1 - jf, (qb_ref, kb_ref, vb_ref, csb_ref, gtb_ref, ob_ref))):
                d = 0 if lower else 1
                rows = slice(j * CHUNK, (j + 1) * CHUNK)
                cs = cs_ref[bi, rows, :]
                gt = gt_ref[bi, j]
                for hp in range(npairs):
                    sl = slice(hp * PAIR, (hp + 1) * PAIR)
                    items.append(dict(
                        q=q_ref[bi, rows, sl], k=k_ref[bi, rows, sl], v=v_ref[bi, rows, sl], cs=cs,
                        grow=gt[d * npairs + hp:d * npairs + hp + 1, :], c0=d * N_HEADS + 2 * hp,
                        sidx=(bi * 2 + d) * npairs + hp, lower=lower, dst=(o_ref, bi, rows, sl)))
        return items

    groups = [[chunk_items(g * DELTA_CHUNKS_PER_ITER + cc) for cc in range(DELTA_CHUNKS_PER_ITER)]
              for g in range(nc // DELTA_CHUNKS_PER_ITER)]
    pending = None
    for rounds in groups:
        prep = _delta_prep([it for items in rounds for it in items])
        if pending is None:
            _interleave(prep)
        else:
            _interleave(prep, _delta_scan(pending, s_ref))
        pending = rounds
    _interleave(_delta_scan(pending, s_ref))


def _delta(q, k, v, cs, gt):
    b, s, _ = q.shape
    tc = TC_DELTA
    nb = s // tc
    ncb = tc // CHUNK
    fwd = lambda w: pl.BlockSpec((b, tc, w), lambda j: (0, j, 0))
    bwd = lambda w: pl.BlockSpec((b, tc, w), lambda j: (0, nb - 1 - j, 0))
    gt_f = pl.BlockSpec((b, ncb, N_HEADS, 2 * CHUNK), lambda j: (0, j, 0, 0))
    gt_b = pl.BlockSpec((b, ncb, N_HEADS, 2 * CHUNK), lambda j: (0, nb - 1 - j, 0, 0))
    return pl.pallas_call(
        _delta_kernel,
        grid=(nb,),
        in_specs=[fwd(D_DN), fwd(D_DN), fwd(D_DN), fwd(LANES), gt_f,
                  bwd(D_DN), bwd(D_DN), bwd(D_DN), bwd(LANES), gt_b],
        out_specs=[fwd(D_DN), bwd(D_DN)],
        out_shape=[jax.ShapeDtypeStruct((b, s, D_DN), F32), jax.ShapeDtypeStruct((b, s, D_DN), F32)],
        scratch_shapes=[pltpu.VMEM((b * N_HEADS, HEAD_DIM, PAIR), F32)],
        compiler_params=pltpu.CompilerParams(dimension_semantics=("arbitrary",),
                                             vmem_limit_bytes=VMEM_LIMIT),
        name="delta",
    )(q, k, v, cs, gt, q, k, v, cs, gt)


def _out_kernel(x_ref, p_ref, of_ref, ob_ref, z_ref, ya_ref, gates_ref,
                gdn_ref, wa_ref, wdn_ref, wo_ref, gffn_ref, w1_ref, w2_ref,
                gple_ref, wpg_ref, wpp_ref, gfin_ref, out_ref):
    o = of_ref[...] + ob_ref[...]
    z = z_ref[...].astype(F32)
    gdn = gdn_ref[...]
    parts = []
    for h in range(N_HEADS):
        sl = slice(h * HEAD_DIM, (h + 1) * HEAD_DIM)
        parts.append(_rms(o[:, sl], gdn) * _silu(z[:, sl]))
    y_dn = jnp.concatenate(parts, axis=1).astype(BF16)
    gates = gates_ref[...].astype(F32)
    merged = (jax.nn.sigmoid(gates[:, :D_MODEL]) * _dot(ya_ref[...], wa_ref[...])
              + jax.nn.sigmoid(gates[:, D_MODEL:]) * _dot(y_dn, wdn_ref[...]))
    h1 = x_ref[...] + _dot(merged.astype(BF16), wo_ref[...])

    n2 = _rms(h1, gffn_ref[...]).astype(BF16)
    h2 = h1
    for c in range(D_FF // FF_CHUNK):
        sl = slice(c * FF_CHUNK, (c + 1) * FF_CHUNK)
        a = jnp.maximum(_dot(n2, w1_ref[:, sl]), 0.0)
        h2 = h2 + _dot((a * a).astype(BF16), w2_ref[sl, :])

    n3 = _rms(h2, gple_ref[...]).astype(BF16)
    gate_p = jax.nn.sigmoid(_dot(n3, wpg_ref[...]))
    h3 = h2 + gate_p * _dot(p_ref[...].astype(BF16), wpp_ref[...])
    out_ref[...] = _rms(h3, gfin_ref[...])


def _out(x2, p2, o_f, o_b, z, ya, gates, gdn, wa, wdn, wo, gffn, w1, w2, gple, wpg, wpp, gfin):
    t = x2.shape[0]
    tm = TM_OUT
    row = lambda w: pl.BlockSpec((tm, w), lambda i: (i, 0))
    weights = (gdn, wa, wdn, wo, gffn, w1, w2, gple, wpg, wpp, gfin)
    return pl.pallas_call(
        _out_kernel,
        grid=(t // tm,),
        in_specs=[row(D_MODEL), row(PLE_DIM), row(D_DN), row(D_DN), row(D_DN), row(D_CONV),
                  row(2 * D_MODEL)] + [_resident(w.shape) for w in weights],
        out_specs=row(D_MODEL),
        out_shape=jax.ShapeDtypeStruct((t, D_MODEL), F32),
        compiler_params=pltpu.CompilerParams(dimension_semantics=("arbitrary",),
                                             vmem_limit_bytes=VMEM_LIMIT),
        name="out",
    )(x2, p2, o_f, o_b, z, ya, gates, *weights)


def kernel(x, p, g_mix, w_in, conv_a, conv_qkv, a_log_f, a_log_b, dt_bias_f, dt_bias_b, g_dn_norm,
           w_a_out, w_dn_out, w_o, g_ffn, w_ff1, w_ff2, g_ple, w_ple_gate, w_ple_proj, g_final):
    b, s, d = x.shape
    depth = p.shape[0]
    assert depth == 1 and d == D_MODEL and s % TC_DELTA == 0 and s % TS_FRONT == 0
    t = b * s
    i = 0

    wt = jnp.transpose(w_in[i]).astype(BF16)

    zeros8 = jnp.zeros((2 * N_HEADS,), F32)
    tail = jnp.zeros((LANES - 4 * N_HEADS,), F32)
    alog = jnp.concatenate([zeros8, a_log_f[i], a_log_b[i], tail])[None, :]
    dtb = jnp.concatenate([zeros8, dt_bias_f[i], dt_bias_b[i], tail])[None, :]
    out_weights = (w_a_out[i], w_dn_out[i], w_o[i], w_ff1[i], w_ff2[i], w_ple_gate[i])
    ya, q, k, v, cs, gt, z, gates, *out_weights = _front(x, g_mix[i][None, :], wt,
                                                         conv_a[i], conv_qkv[i], alog, dtb, out_weights)
    wa, wdn, wo, w1, w2, wpg = out_weights
    wpp = w_ple_proj[i].astype(BF16)

    gt = gt.reshape(b, s // CHUNK, N_HEADS, 2 * CHUNK)
    o_f, o_b = _delta(q, k, v, cs, gt)

    out = _out(x.reshape(t, d), p[i].reshape(t, PLE_DIM), o_f.reshape(t, D_DN), o_b.reshape(t, D_DN),
               z.reshape(t, D_DN), ya.reshape(t, D_CONV), gates.reshape(t, 2 * D_MODEL),
               g_dn_norm[i][None, :], wa, wdn, wo, g_ffn[i][None, :], w1, w2,
               g_ple[i][None, :], wpg, wpp, g_final[None, :])
    return out.reshape(b, s, d)
```

```python
import functools

import jax
import jax.numpy as jnp
from jax import lax
from jax.experimental import pallas as pl
from jax.experimental.pallas import tpu as pltpu

F32 = jnp.float32
BF16 = jnp.bfloat16

D_MODEL = 1024
PLE_DIM = 256
D_CONV = 512
N_HEADS = 4
HEAD_DIM = 128
D_DN = N_HEADS * HEAD_DIM
CHUNK = 64
D_FF = 4 * D_MODEL
EPS = 1e-6

LANES = 128
VMEM_LIMIT = 56 * 1024 * 1024

TS_FRONT = 512
TC_DELTA = 512
DELTA_CHUNKS_PER_ITER = 2
TM_OUT = 512
FF_CHUNK = 1024

W_PA = 0
W_QKV = W_PA + 3 * D_CONV
W_Z = W_QKV + 3 * D_DN
W_SM = W_Z + D_DN
W_GATES = W_SM + 4 * N_HEADS
W_END = W_GATES + 2 * D_MODEL
W_GROUPS = ((W_PA, W_QKV), (W_QKV, W_Z), (W_Z, W_SM), (W_SM, W_SM + LANES), (W_GATES, W_END))
W_STAGE_ROWS = 512

CS_BETA, CS_G, CS_EG, CS_EKG, CS_EGT = 0, 8, 16, 24, 32


def _resident(shape):
    nd = len(shape)
    return pl.BlockSpec(shape, lambda *_: (0,) * nd, pipeline_mode=pl.Buffered(1))


def _rms(x, g):
    return x * lax.rsqrt(jnp.mean(x * x, axis=-1, keepdims=True) + EPS) * g


def _dot(a, b):
    return jnp.dot(a, b, preferred_element_type=F32)


def _silu(x):
    return x * jax.nn.sigmoid(x)


HALO = 8


def _conv3(ext, w, ts):
    n = ext.shape[0]
    up = pltpu.roll(ext, 1, 0)[HALO:HALO + ts]
    dn = pltpu.roll(ext, n - 1, 0)[HALO:HALO + ts]
    return up * w[0:1, :] + ext[HALO:HALO + ts] * w[1:2, :] + dn * w[2:3, :]


def _chunk_prefix_sums(g):
    ts = g.shape[0]
    hi = g.astype(BF16).astype(F32)
    r1 = g - hi
    mid = r1.astype(BF16).astype(F32)
    lo = (r1 - mid).astype(BF16).astype(F32)
    packed = (hi + pltpu.roll(mid, 16, 1) + pltpu.roll(lo, 32, 1)).astype(BF16)
    r = lax.broadcasted_iota(jnp.int32, (CHUNK, CHUNK), 0)
    c = lax.broadcasted_iota(jnp.int32, (CHUNK, CHUNK), 1)
    tri = jnp.where(c <= r, 1.0, 0.0).astype(BF16)
    res = jnp.concatenate([_dot(tri, packed[n * CHUNK:(n + 1) * CHUNK]) for n in range(ts // CHUNK)], axis=0)
    return res + pltpu.roll(res, LANES - 16, 1) + pltpu.roll(res, LANES - 32, 1)


def _stage_projection_weights(wt_hbm, w_refs, stage_ref, sem):
    chunks = []
    for (lo, hi), dst in zip(W_GROUPS, w_refs):
        for r0 in range(lo, hi, W_STAGE_ROWS):
            chunks.append((r0, min(W_STAGE_ROWS, hi - r0), dst, r0 - lo))

    def copy(i):
        r0, n, _, _ = chunks[i]
        return pltpu.make_async_copy(wt_hbm.at[pl.ds(r0, n), :], stage_ref.at[i % 2, pl.ds(0, n), :], sem.at[i % 2])

    copy(0).start()
    for i, (_, n, dst, c0) in enumerate(chunks):
        if i + 1 < len(chunks):
            copy(i + 1).start()
        copy(i).wait()
        dst[:, c0:c0 + n] = stage_ref[i % 2, 0:n, :].astype(BF16).T


def _front_kernel(x_ref, xp_ref, xn_ref, wt_hbm, gmix_ref,
                  conv_a_ref, conv_qkv_ref, alog_ref, dtb_ref, *rest):
    nw = (len(rest) - 10 - len(W_GROUPS)) // 2
    wsrc_refs, (ya_ref, q_ref, k_ref, v_ref, cs_ref, gt_ref, z_ref, gates_ref), wdst_refs = (
        rest[:nw], rest[nw:nw + 8], rest[nw + 8:2 * nw + 8])
    wpa_ref, wqkv_ref, wz_ref, wsm_ref, wg_ref = w_refs = rest[2 * nw + 8:2 * nw + 8 + len(W_GROUPS)]
    stage_ref, stage_sem = rest[2 * nw + 8 + len(W_GROUPS):]
    j = pl.program_id(1)
    nj = pl.num_programs(1)
    ts = TS_FRONT

    @pl.when((pl.program_id(0) == 0) & (j == 0))
    def _():
        _stage_projection_weights(wt_hbm, w_refs, stage_ref, stage_sem)

    has_prev = (j > 0).astype(F32)
    has_next = (j < nj - 1).astype(F32)
    inner = slice(HALO, HALO + ts)

    gm = gmix_ref[...]
    n_cur = _rms(x_ref[0], gm)
    n_ext = jnp.concatenate([_rms(xp_ref[0], gm) * has_prev, n_cur, _rms(xn_ref[0], gm) * has_next],
                            axis=0).astype(BF16)
    n_b = n_cur.astype(BF16)

    sm = _dot(n_b, wsm_ref[...])
    lane = lax.broadcasted_iota(jnp.int32, sm.shape, 1)
    beta = jax.nn.sigmoid(sm)
    xs = sm + dtb_ref[...]
    softplus = jnp.maximum(xs, 0.0) + jnp.log1p(jnp.exp(-jnp.abs(xs)))
    g = jnp.where((lane >= 8) & (lane < 16), -jnp.exp(alog_ref[...]) * softplus, 0.0)

    pa = _dot(n_ext, wpa_ref[...])
    u = pa[:, :D_CONV] * pa[:, 2 * D_CONV:]
    ya = pa[inner, D_CONV:2 * D_CONV] * _conv3(u, conv_a_ref[...], ts)
    ya_ref[0] = ya.astype(ya_ref.dtype)

    pre = _chunk_prefix_sums(g)
    pre3 = pre.reshape(ts // CHUNK, CHUNK, LANES)
    tot = jnp.broadcast_to(pre3[:, CHUNK - 1:CHUNK, :], pre3.shape).reshape(ts, LANES)
    gc = jnp.where(lane < 12, pre, tot - pre + g)
    eg = jnp.exp(gc)
    ekg = jnp.exp(tot - gc)
    egt = jnp.exp(tot)

    g8 = pltpu.roll(gc, LANES - 8, 1)
    cs = jnp.where(lane < 8, beta, 0.0)
    cs = jnp.where((lane >= 8) & (lane < 16), gc, cs)
    cs = jnp.where((lane >= 16) & (lane < 24), pltpu.roll(eg, 8, 1), cs)
    cs = jnp.where((lane >= 24) & (lane < 32), pltpu.roll(ekg, 16, 1), cs)
    cs = jnp.where((lane >= 32) & (lane < 40), pltpu.roll(egt, 24, 1), cs)
    cs_ref[0] = cs

    g8t = g8.T
    for n in range(ts // CHUNK):
        gt_ref[0, n] = g8t[0:8, n * CHUNK:(n + 1) * CHUNK]

    for gi, (ref, scale) in enumerate(((q_ref, HEAD_DIM ** -0.5), (k_ref, None), (v_ref, None))):
        sl = slice(gi * D_DN, (gi + 1) * D_DN)
        c = _silu(_conv3(_dot(n_ext, wqkv_ref[:, sl]), conv_qkv_ref[:, sl], ts))
        if ref is v_ref:
            v_ref[0] = c
            continue
        for h in range(N_HEADS):
            hs = slice(h * HEAD_DIM, (h + 1) * HEAD_DIM)
            ch = c[:, hs]
            cn = ch * lax.rsqrt(jnp.sum(ch * ch, axis=-1, keepdims=True) + EPS)
            ref[0, :, hs] = cn if scale is None else cn * scale

    z_ref[0] = _dot(n_b, wz_ref[...]).astype(z_ref.dtype)
    gates_ref[0] = _dot(n_b, wg_ref[...]).astype(gates_ref.dtype)

    for src, dst in zip(wsrc_refs, wdst_refs):
        dst[...] = src[...].astype(dst.dtype)


def _front(x, g_mix, wt, conv_a, conv_qkv, alog, dtb, cast_weights):
    b, s, d = x.shape
    ts = TS_FRONT
    nj = s // ts
    r8 = ts // HALO
    n8 = s // HALO

    def cur(w):
        return pl.BlockSpec((1, ts, w), lambda i, j: (i, j, 0))

    prev = pl.BlockSpec((1, HALO, d), lambda i, j: (i, jnp.maximum(j * r8 - 1, 0), 0))
    nxt = pl.BlockSpec((1, HALO, d), lambda i, j: (i, jnp.minimum((j + 1) * r8, n8 - 1), 0))
    consts = (g_mix, conv_a, conv_qkv, alog, dtb)
    slab = lambda a: pl.BlockSpec((a.shape[0] // (b * nj), a.shape[1]), lambda i, j: (i * nj + j, 0))
    return pl.pallas_call(
        _front_kernel,
        grid=(b, nj),
        in_specs=[cur(d), prev, nxt, pl.BlockSpec(memory_space=pl.ANY)] + [_resident(c.shape) for c in consts]
        + [slab(a) for a in cast_weights],
        out_specs=[cur(D_CONV), cur(D_DN), cur(D_DN), cur(D_DN), cur(LANES),
                   pl.BlockSpec((1, ts // CHUNK, 8, CHUNK), lambda i, j: (i, j, 0, 0)),
                   cur(D_DN), cur(2 * D_MODEL)] + [slab(a) for a in cast_weights],
        out_shape=[jax.ShapeDtypeStruct((b, s, D_CONV), BF16),
                   jax.ShapeDtypeStruct((b, s, D_DN), F32),
                   jax.ShapeDtypeStruct((b, s, D_DN), F32),
                   jax.ShapeDtypeStruct((b, s, D_DN), F32),
                   jax.ShapeDtypeStruct((b, s, LANES), F32),
                   jax.ShapeDtypeStruct((b, s // CHUNK, 8, CHUNK), F32),
                   jax.ShapeDtypeStruct((b, s, D_DN), BF16),
                   jax.ShapeDtypeStruct((b, s, 2 * D_MODEL), BF16)]
        + [jax.ShapeDtypeStruct(a.shape, BF16) for a in cast_weights],
        scratch_shapes=[pltpu.VMEM((d, hi - lo), BF16) for lo, hi in W_GROUPS]
        + [pltpu.VMEM((2, W_STAGE_ROWS, d), F32), pltpu.SemaphoreType.DMA((2,))],
        compiler_params=pltpu.CompilerParams(dimension_semantics=("arbitrary", "arbitrary"),
                                             vmem_limit_bytes=VMEM_LIMIT),
        name="front",
    )(x, x, x, wt, *consts, *cast_weights)


PAIR = 2 * HEAD_DIM


def _block_diag(x, half):
    lane = lax.broadcasted_iota(jnp.int32, x.shape, 1)
    zero = jnp.zeros_like(x)
    return jnp.concatenate([jnp.where(lane < half, x, zero), jnp.where(lane >= half, x, zero)], axis=0)


def _delta_prep(items):
    r = lax.broadcasted_iota(jnp.int32, (CHUNK, 2 * CHUNK), 0)
    lane = lax.broadcasted_iota(jnp.int32, (CHUNK, 2 * CHUNK), 1)
    c = jnp.where(lane < CHUNK, lane, lane - CHUNK)
    first = lane < CHUNK
    eye = jnp.where(r == c, 1.0, 0.0)

    def col256(cs, base, c0):
        return jnp.concatenate([jnp.broadcast_to(cs[:, base + c0:base + c0 + 1], (CHUNK, HEAD_DIM)),
                                jnp.broadcast_to(cs[:, base + c0 + 1:base + c0 + 2], (CHUNK, HEAD_DIM))], axis=1)

    for it in items:
        cs, c0 = it["cs"], it["c0"]
        it["beta"] = col256(cs, CS_BETA, c0)
        it["eg"] = col256(cs, CS_EG, c0)
        it["kb"] = it["k"] * it["beta"]
        lhs = jnp.concatenate([it["kb"], it["q"]], axis=0).astype(BF16)
        it["kq"] = lax.dot_general(lhs, _block_diag(it["k"].astype(BF16), HEAD_DIM),
                                   (((1,), (1,)), ((), ())), preferred_element_type=F32)
    yield

    for it in items:
        cs, c0 = it["cs"], it["c0"]
        incl = (r >= c) if it["lower"] else (r <= c)
        strict = (r > c) if it["lower"] else (r < c)
        gcol = jnp.where(first, cs[:, CS_G + c0:CS_G + c0 + 1], cs[:, CS_G + c0 + 1:CS_G + c0 + 2])
        decay = jnp.where(incl, jnp.exp(jnp.where(incl, gcol - it["grow"], 0.0)), 0.0)
        a = jnp.where(strict, it["kq"][:CHUNK] * decay, 0.0)
        it["intra"] = (it["kq"][CHUNK:] * decay).astype(BF16)
        it["p"] = eye - a
        it["a"] = a.astype(BF16)

    for it in items:
        it["y"] = _dot(it["a"], _block_diag(it["a"], CHUNK)).astype(BF16)
    yield
    steps = 2
    while 2 * steps < CHUNK:
        steps *= 2
        for it in items:
            res = _dot(jnp.concatenate([it["p"].astype(BF16), it["y"]], axis=0), _block_diag(it["y"], CHUNK))
            it["p"] = it["p"] + res[:CHUNK]
            it["y"] = res[CHUNK:].astype(BF16)
        yield
    for it in items:
        it["p"] = it["p"] + _dot(it["p"].astype(BF16), _block_diag(it["y"], CHUNK))
    yield

    for it in items:
        vb = (it["v"] * it["beta"]).astype(BF16)
        kbg = (it["kb"] * it["eg"]).astype(BF16)
        rhs = jnp.concatenate([_block_diag(vb, HEAD_DIM), _block_diag(kbg, HEAD_DIM)], axis=1)
        uw = _dot(it["p"].astype(BF16), rhs)
        it["u"] = uw[:, :PAIR]
        it["w"] = uw[:, PAIR:]
        it["qg"] = it["q"] * it["eg"]
        kg = (it["k"] * col256(it["cs"], CS_EKG, it["c0"])).astype(BF16)
        it["kgs"] = jnp.concatenate([kg[:, :HEAD_DIM], kg[:, HEAD_DIM:]], axis=0)
    yield


def _delta_scan(rounds, s_ref):
    for items in rounds:
        for it in items:
            it["state"] = s_ref[it["sidx"]]
            sb = it["state"].astype(BF16)
            ws = []
            for h in range(2):
                sl = slice(h * HEAD_DIM, (h + 1) * HEAD_DIM)
                lhs = jnp.concatenate([it["w"][:, sl], it["qg"][:, sl]], axis=0).astype(BF16)
                ws.append(_dot(lhs, sb[:, sl]))
            it["ws_top"] = jnp.concatenate([ws[0][:CHUNK], ws[1][:CHUNK]], axis=1)
            it["ws_bot"] = jnp.concatenate([ws[0][CHUNK:], ws[1][CHUNK:]], axis=1)
        yield

        for it in items:
            cs, c0 = it["cs"], it["c0"]
            v_new = _block_diag((it["u"] - it["ws_top"]).astype(BF16), HEAD_DIM)
            o_ref, bi, rows, sl = it["dst"]
            o_ref[bi, rows, sl] = it["ws_bot"] + _dot(it["intra"], v_new)
            upd = lax.dot_general(it["kgs"], v_new, (((0,), (0,)), ((), ())), preferred_element_type=F32)
            egt = jnp.concatenate([jnp.broadcast_to(cs[0:1, CS_EGT + c0:CS_EGT + c0 + 1], (1, HEAD_DIM)),
                                   jnp.broadcast_to(cs[0:1, CS_EGT + c0 + 1:CS_EGT + c0 + 2], (1, HEAD_DIM))], axis=1)
            s_ref[it["sidx"]] = it["state"] * egt + upd
        yield


def _interleave(*stage_gens):
    live = list(stage_gens)
    while live:
        for g in list(live):
            if next(g, StopIteration) is StopIteration:
                live.remove(g)


def _delta_kernel(qf_ref, kf_ref, vf_ref, csf_ref, gtf_ref, qb_ref, kb_ref, vb_ref, csb_ref, gtb_ref,
                  of_ref, ob_ref, s_ref):
    @pl.when(pl.program_id(0) == 0)
    def _():
        s_ref[...] = jnp.zeros_like(s_ref)

    nbatch = qf_ref.shape[0]
    nc = TC_DELTA // CHUNK
    npairs = N_HEADS // 2

    def chunk_items(jf):
        items = []
        for bi in range(nbatch):
            for lower, j, (q_ref, k_ref, v_ref, cs_ref, gt_ref, o_ref) in (
                    (True, jf, (qf_ref, kf_ref, vf_ref, csf_ref, gtf_ref, of_ref)),
                    (False, nc - 1 - jf, (qb_ref, kb_ref, vb_ref, csb_ref, gtb_ref, ob_ref))):
                d = 0 if lower else 1
                rows = slice(j * CHUNK, (j + 1) * CHUNK)
                cs = cs_ref[bi, rows, :]
                gt = gt_ref[bi, j]
                for hp in range(npairs):
                    sl = slice(hp * PAIR, (hp + 1) * PAIR)
                    items.append(dict(
                        q=q_ref[bi, rows, sl], k=k_ref[bi, rows, sl], v=v_ref[bi, rows, sl], cs=cs,
                        grow=gt[d * npairs + hp:d * npairs + hp + 1, :], c0=d * N_HEADS + 2 * hp,
                        sidx=(bi * 2 + d) * npairs + hp, lower=lower, dst=(o_ref, bi, rows, sl)))
        return items

    groups = [[chunk_items(g * DELTA_CHUNKS_PER_ITER + cc) for cc in range(DELTA_CHUNKS_PER_ITER)]
              for g in range(nc // DELTA_CHUNKS_PER_ITER)]
    pending = None
    for rounds in groups:
        prep = _delta_prep([it for items in rounds for it in items])
        if pending is None:
            _interleave(prep)
        else:
            _interleave(prep, _delta_scan(pending, s_ref))
        pending = rounds
    _interleave(_delta_scan(pending, s_ref))


def _delta(q, k, v, cs, gt):
    b, s, _ = q.shape
    tc = TC_DELTA
    nb = s // tc
    ncb = tc // CHUNK
    fwd = lambda w: pl.BlockSpec((b, tc, w), lambda j: (0, j, 0))
    bwd = lambda w: pl.BlockSpec((b, tc, w), lambda j: (0, nb - 1 - j, 0))
    gt_f = pl.BlockSpec((b, ncb, N_HEADS, 2 * CHUNK), lambda j: (0, j, 0, 0))
    gt_b = pl.BlockSpec((b, ncb, N_HEADS, 2 * CHUNK), lambda j: (0, nb - 1 - j, 0, 0))
    return pl.pallas_call(
        _delta_kernel,
        grid=(nb,),
        in_specs=[fwd(D_DN), fwd(D_DN), fwd(D_DN), fwd(LANES), gt_f,
                  bwd(D_DN), bwd(D_DN), bwd(D_DN), bwd(LANES), gt_b],
        out_specs=[fwd(D_DN), bwd(D_DN)],
        out_shape=[jax.ShapeDtypeStruct((b, s, D_DN), F32), jax.ShapeDtypeStruct((b, s, D_DN), F32)],
        scratch_shapes=[pltpu.VMEM((b * N_HEADS, HEAD_DIM, PAIR), F32)],
        compiler_params=pltpu.CompilerParams(dimension_semantics=("arbitrary",),
                                             vmem_limit_bytes=VMEM_LIMIT),
        name="delta",
    )(q, k, v, cs, gt, q, k, v, cs, gt)


def _out_kernel(x_ref, p_ref, of_ref, ob_ref, z_ref, ya_ref, gates_ref,
                gdn_ref, wa_ref, wdn_ref, wo_ref, gffn_ref, w1_ref, w2_ref,
                gple_ref, wpg_ref, wpp_ref, gfin_ref, out_ref):
    o = of_ref[...] + ob_ref[...]
    z = z_ref[...].astype(F32)
    gdn = gdn_ref[...]
    parts = []
    for h in range(N_HEADS):
        sl = slice(h * HEAD_DIM, (h + 1) * HEAD_DIM)
        parts.append(_rms(o[:, sl], gdn) * _silu(z[:, sl]))
    y_dn = jnp.concatenate(parts, axis=1).astype(BF16)
    gates = gates_ref[...].astype(F32)
    merged = (jax.nn.sigmoid(gates[:, :D_MODEL]) * _dot(ya_ref[...], wa_ref[...])
              + jax.nn.sigmoid(gates[:, D_MODEL:]) * _dot(y_dn, wdn_ref[...]))
    h1 = x_ref[...] + _dot(merged.astype(BF16), wo_ref[...])

    n2 = _rms(h1, gffn_ref[...]).astype(BF16)
    h2 = h1
    for c in range(D_FF // FF_CHUNK):
        sl = slice(c * FF_CHUNK, (c + 1) * FF_CHUNK)
        a = jnp.maximum(_dot(n2, w1_ref[:, sl]), 0.0)
        h2 = h2 + _dot((a * a).astype(BF16), w2_ref[sl, :])

    n3 = _rms(h2, gple_ref[...]).astype(BF16)
    gate_p = jax.nn.sigmoid(_dot(n3, wpg_ref[...]))
    h3 = h2 + gate_p * _dot(p_ref[...].astype(BF16), wpp_ref[...])
    out_ref[...] = _rms(h3, gfin_ref[...])


def _out(x2, p2, o_f, o_b, z, ya, gates, gdn, wa, wdn, wo, gffn, w1, w2, gple, wpg, wpp, gfin):
    t = x2.shape[0]
    tm = TM_OUT
    row = lambda w: pl.BlockSpec((tm, w), lambda i: (i, 0))
    weights = (gdn, wa, wdn, wo, gffn, w1, w2, gple, wpg, wpp, gfin)
    return pl.pallas_call(
        _out_kernel,
        grid=(t // tm,),
        in_specs=[row(D_MODEL), row(PLE_DIM), row(D_DN), row(D_DN), row(D_DN), row(D_CONV),
                  row(2 * D_MODEL)] + [_resident(w.shape) for w in weights],
        out_specs=row(D_MODEL),
        out_shape=jax.ShapeDtypeStruct((t, D_MODEL), F32),
        compiler_params=pltpu.CompilerParams(dimension_semantics=("arbitrary",),
                                             vmem_limit_bytes=VMEM_LIMIT),
        name="out",
    )(x2, p2, o_f, o_b, z, ya, gates, *weights)


def kernel(x, p, g_mix, w_in, conv_a, conv_qkv, a_log_f, a_log_b, dt_bias_f, dt_bias_b, g_dn_norm,
           w_a_out, w_dn_out, w_o, g_ffn, w_ff1, w_ff2, g_ple, w_ple_gate, w_ple_proj, g_final):
    b, s, d = x.shape
    depth = p.shape[0]
    assert depth == 1 and d == D_MODEL and s % TC_DELTA == 0 and s % TS_FRONT == 0
    t = b * s
    i = 0

    wt = jnp.transpose(w_in[i])

    zeros8 = jnp.zeros((2 * N_HEADS,), F32)
    tail = jnp.zeros((LANES - 4 * N_HEADS,), F32)
    alog = jnp.concatenate([zeros8, a_log_f[i], a_log_b[i], tail])[None, :]
    dtb = jnp.concatenate([zeros8, dt_bias_f[i], dt_bias_b[i], tail])[None, :]
    out_weights = (w_a_out[i], w_dn_out[i], w_o[i], w_ff1[i], w_ff2[i], w_ple_gate[i])
    ya, q, k, v, cs, gt, z, gates, *out_weights = _front(x, g_mix[i][None, :], wt,
                                                         conv_a[i], conv_qkv[i], alog, dtb, out_weights)
    wa, wdn, wo, w1, w2, wpg = out_weights
    wpp = w_ple_proj[i].astype(BF16)

    gt = gt.reshape(b, s // CHUNK, N_HEADS, 2 * CHUNK)
    o_f, o_b = _delta(q, k, v, cs, gt)

    out = _out(x.reshape(t, d), p[i].reshape(t, PLE_DIM), o_f.reshape(t, D_DN), o_b.reshape(t, D_DN),
               z.reshape(t, D_DN), ya.reshape(t, D_CONV), gates.reshape(t, 2 * D_MODEL),
               g_dn_norm[i][None, :], wa, wdn, wo, g_ffn[i][None, :], w1, w2,
               g_ple[i][None, :], wpg, wpp, g_final[None, :])
    return out.reshape(b, s, d)
```

```python
import functools

import jax
import jax.numpy as jnp
from jax import lax
from jax.experimental import pallas as pl
from jax.experimental.pallas import tpu as pltpu

F32 = jnp.float32
BF16 = jnp.bfloat16

D_MODEL = 1024
PLE_DIM = 256
D_CONV = 512
N_HEADS = 4
HEAD_DIM = 128
D_DN = N_HEADS * HEAD_DIM
CHUNK = 64
D_FF = 4 * D_MODEL
EPS = 1e-6

LANES = 128
VMEM_LIMIT = 56 * 1024 * 1024

TS_FRONT = 512
TC_DELTA = 512
DELTA_CHUNKS_PER_ITER = 2
TM_OUT = 512
FF_CHUNK = 1024

W_PA = 0
W_QKV = W_PA + 3 * D_CONV
W_Z = W_QKV + 3 * D_DN
W_SM = W_Z + D_DN
W_GATES = W_SM + 4 * N_HEADS
W_END = W_GATES + 2 * D_MODEL
W_GROUPS = ((W_PA, W_QKV), (W_QKV, W_Z), (W_Z, W_SM), (W_SM, W_SM + LANES), (W_GATES, W_END))
W_STAGE_ROWS = 512

CS_BETA, CS_G, CS_EG, CS_EKG, CS_EGT = 0, 8, 16, 24, 32


def _resident(shape):
    nd = len(shape)
    return pl.BlockSpec(shape, lambda *_: (0,) * nd, pipeline_mode=pl.Buffered(1))


def _rms(x, g):
    return x * lax.rsqrt(jnp.mean(x * x, axis=-1, keepdims=True) + EPS) * g


def _dot(a, b):
    return jnp.dot(a, b, preferred_element_type=F32)


def _silu(x):
    return x * jax.nn.sigmoid(x)


HALO = 8


def _conv3(ext, w, ts):
    n = ext.shape[0]
    up = pltpu.roll(ext, 1, 0)[HALO:HALO + ts]
    dn = pltpu.roll(ext, n - 1, 0)[HALO:HALO + ts]
    return up * w[0:1, :] + ext[HALO:HALO + ts] * w[1:2, :] + dn * w[2:3, :]


def _chunk_prefix_sums(g):
    ts = g.shape[0]
    hi = g.astype(BF16).astype(F32)
    r1 = g - hi
    mid = r1.astype(BF16).astype(F32)
    lo = (r1 - mid).astype(BF16).astype(F32)
    packed = (hi + pltpu.roll(mid, 16, 1) + pltpu.roll(lo, 32, 1)).astype(BF16)
    r = lax.broadcasted_iota(jnp.int32, (CHUNK, CHUNK), 0)
    c = lax.broadcasted_iota(jnp.int32, (CHUNK, CHUNK), 1)
    tri = jnp.where(c <= r, 1.0, 0.0).astype(BF16)
    res = jnp.concatenate([_dot(tri, packed[n * CHUNK:(n + 1) * CHUNK]) for n in range(ts // CHUNK)], axis=0)
    return res + pltpu.roll(res, LANES - 16, 1) + pltpu.roll(res, LANES - 32, 1)


def _stage_projection_weights(wt_hbm, w_refs, stage_ref, sem):
    chunks = []
    for (lo, hi), dst in zip(W_GROUPS, w_refs):
        for r0 in range(lo, hi, W_STAGE_ROWS):
            chunks.append((r0, min(W_STAGE_ROWS, hi - r0), dst, r0 - lo))

    def copy(i):
        r0, n, _, _ = chunks[i]
        return pltpu.make_async_copy(wt_hbm.at[pl.ds(r0, n), :], stage_ref.at[i % 2, pl.ds(0, n), :], sem.at[i % 2])

    copy(0).start()
    for i, (_, n, dst, c0) in enumerate(chunks):
        if i + 1 < len(chunks):
            copy(i + 1).start()
        copy(i).wait()
        dst[:, c0:c0 + n] = stage_ref[i % 2, 0:n, :].astype(BF16).T


def _front_kernel(x_ref, xp_ref, xn_ref, wt_hbm, gmix_ref,
                  conv_a_ref, conv_qkv_ref, alog_ref, dtb_ref, *rest):
    nw = (len(rest) - 10 - len(W_GROUPS)) // 2
    wsrc_refs, (ya_ref, q_ref, k_ref, v_ref, cs_ref, gt_ref, z_ref, gates_ref), wdst_refs = (
        rest[:nw], rest[nw:nw + 8], rest[nw + 8:2 * nw + 8])
    wpa_ref, wqkv_ref, wz_ref, wsm_ref, wg_ref = w_refs = rest[2 * nw + 8:2 * nw + 8 + len(W_GROUPS)]
    stage_ref, stage_sem = rest[2 * nw + 8 + len(W_GROUPS):]
    j = pl.program_id(1)
    nj = pl.num_programs(1)
    ts = TS_FRONT

    @pl.when((pl.program_id(0) == 0) & (j == 0))
    def _():
        _stage_projection_weights(wt_hbm, w_refs, stage_ref, stage_sem)

    has_prev = (j > 0).astype(F32)
    has_next = (j < nj - 1).astype(F32)
    inner = slice(HALO, HALO + ts)

    gm = gmix_ref[...]
    n_cur = _rms(x_ref[0], gm)
    n_ext = jnp.concatenate([_rms(xp_ref[0], gm) * has_prev, n_cur, _rms(xn_ref[0], gm) * has_next],
                            axis=0).astype(BF16)
    n_b = n_cur.astype(BF16)

    sm = _dot(n_b, wsm_ref[...])
    lane = lax.broadcasted_iota(jnp.int32, sm.shape, 1)
    beta = jax.nn.sigmoid(sm)
    xs = sm + dtb_ref[...]
    softplus = jnp.maximum(xs, 0.0) + jnp.log1p(jnp.exp(-jnp.abs(xs)))
    g = jnp.where((lane >= 8) & (lane < 16), -jnp.exp(alog_ref[...]) * softplus, 0.0)

    pa = _dot(n_ext, wpa_ref[...])
    u = pa[:, :D_CONV] * pa[:, 2 * D_CONV:]
    ya = pa[inner, D_CONV:2 * D_CONV] * _conv3(u, conv_a_ref[...], ts)
    ya_ref[0] = ya.astype(ya_ref.dtype)

    pre = _chunk_prefix_sums(g)
    pre3 = pre.reshape(ts // CHUNK, CHUNK, LANES)
    tot = jnp.broadcast_to(pre3[:, CHUNK - 1:CHUNK, :], pre3.shape).reshape(ts, LANES)
    gc = jnp.where(lane < 12, pre, tot - pre + g)
    eg = jnp.exp(gc)
    ekg = jnp.exp(tot - gc)
    egt = jnp.exp(tot)

    g8 = pltpu.roll(gc, LANES - 8, 1)
    cs = jnp.where(lane < 8, beta, 0.0)
    cs = jnp.where((lane >= 8) & (lane < 16), gc, cs)
    cs = jnp.where((lane >= 16) & (lane < 24), pltpu.roll(eg, 8, 1), cs)
    cs = jnp.where((lane >= 24) & (lane < 32), pltpu.roll(ekg, 16, 1), cs)
    cs = jnp.where((lane >= 32) & (lane < 40), pltpu.roll(egt, 24, 1), cs)
    cs_ref[0] = cs

    g8t = g8.T
    for n in range(ts // CHUNK):
        gt_ref[0, n] = g8t[0:8, n * CHUNK:(n + 1) * CHUNK]

    for gi, (ref, scale) in enumerate(((q_ref, HEAD_DIM ** -0.5), (k_ref, None), (v_ref, None))):
        sl = slice(gi * D_DN, (gi + 1) * D_DN)
        c = _silu(_conv3(_dot(n_ext, wqkv_ref[:, sl]), conv_qkv_ref[:, sl], ts))
        if ref is v_ref:
            v_ref[0] = c
            continue
        for h in range(N_HEADS):
            hs = slice(h * HEAD_DIM, (h + 1) * HEAD_DIM)
            ch = c[:, hs]
            cn = ch * lax.rsqrt(jnp.sum(ch * ch, axis=-1, keepdims=True) + EPS)
            ref[0, :, hs] = cn if scale is None else cn * scale

    z_ref[0] = _dot(n_b, wz_ref[...]).astype(z_ref.dtype)
    gates_ref[0] = _dot(n_b, wg_ref[...]).astype(gates_ref.dtype)

    for src, dst in zip(wsrc_refs, wdst_refs):
        dst[...] = src[...].astype(dst.dtype)


def _front(x, g_mix, wt, conv_a, conv_qkv, alog, dtb, cast_weights):
    b, s, d = x.shape
    ts = TS_FRONT
    nj = s // ts
    r8 = ts // HALO
    n8 = s // HALO

    def cur(w):
        return pl.BlockSpec((1, ts, w), lambda i, j: (i, j, 0))

    prev = pl.BlockSpec((1, HALO, d), lambda i, j: (i, jnp.maximum(j * r8 - 1, 0), 0))
    nxt = pl.BlockSpec((1, HALO, d), lambda i, j: (i, jnp.minimum((j + 1) * r8, n8 - 1), 0))
    consts = (g_mix, conv_a, conv_qkv, alog, dtb)
    slab = lambda a: pl.BlockSpec((a.shape[0] // (b * nj), a.shape[1]), lambda i, j: (i * nj + j, 0))
    return pl.pallas_call(
        _front_kernel,
        grid=(b, nj),
        in_specs=[cur(d), prev, nxt, pl.BlockSpec(memory_space=pl.ANY)] + [_resident(c.shape) for c in consts]
        + [slab(a) for a in cast_weights],
        out_specs=[cur(D_CONV), cur(D_DN), cur(D_DN), cur(D_DN), cur(LANES),
                   pl.BlockSpec((1, ts // CHUNK, 8, CHUNK), lambda i, j: (i, j, 0, 0)),
                   cur(D_DN), cur(2 * D_MODEL)] + [slab(a) for a in cast_weights],
        out_shape=[jax.ShapeDtypeStruct((b, s, D_CONV), BF16),
                   jax.ShapeDtypeStruct((b, s, D_DN), F32),
                   jax.ShapeDtypeStruct((b, s, D_DN), F32),
                   jax.ShapeDtypeStruct((b, s, D_DN), F32),
                   jax.ShapeDtypeStruct((b, s, LANES), F32),
                   jax.ShapeDtypeStruct((b, s // CHUNK, 8, CHUNK), F32),
                   jax.ShapeDtypeStruct((b, s, D_DN), BF16),
                   jax.ShapeDtypeStruct((b, s, 2 * D_MODEL), BF16)]
        + [jax.ShapeDtypeStruct(a.shape, BF16) for a in cast_weights],
        scratch_shapes=[pltpu.VMEM((d, hi - lo), BF16) for lo, hi in W_GROUPS]
        + [pltpu.VMEM((2, W_STAGE_ROWS, d), F32), pltpu.SemaphoreType.DMA((2,))],
        compiler_params=pltpu.CompilerParams(dimension_semantics=("arbitrary", "arbitrary"),
                                             vmem_limit_bytes=VMEM_LIMIT),
        name="front",
    )(x, x, x, wt, *consts, *cast_weights)


PAIR = 2 * HEAD_DIM


def _block_diag(x, half):
    lane = lax.broadcasted_iota(jnp.int32, x.shape, 1)
    zero = jnp.zeros_like(x)
    return jnp.concatenate([jnp.where(lane < half, x, zero), jnp.where(lane >= half, x, zero)], axis=0)


def _delta_prep(items):
    r = lax.broadcasted_iota(jnp.int32, (CHUNK, 2 * CHUNK), 0)
    lane = lax.broadcasted_iota(jnp.int32, (CHUNK, 2 * CHUNK), 1)
    c = jnp.where(lane < CHUNK, lane, lane - CHUNK)
    first = lane < CHUNK
    eye = jnp.where(r == c, 1.0, 0.0)

    def col256(cs, base, c0):
        return jnp.concatenate([jnp.broadcast_to(cs[:, base + c0:base + c0 + 1], (CHUNK, HEAD_DIM)),
                                jnp.broadcast_to(cs[:, base + c0 + 1:base + c0 + 2], (CHUNK, HEAD_DIM))], axis=1)

    for it in items:
        cs, c0 = it["cs"], it["c0"]
        it["beta"] = col256(cs, CS_BETA, c0)
        it["eg"] = col256(cs, CS_EG, c0)
        it["kb"] = it["k"] * it["beta"]
        lhs = jnp.concatenate([it["kb"], it["q"]], axis=0).astype(BF16)
        it["kq"] = lax.dot_general(lhs, _block_diag(it["k"].astype(BF16), HEAD_DIM),
                                   (((1,), (1,)), ((), ())), preferred_element_type=F32)
    yield

    for it in items:
        cs, c0 = it["cs"], it["c0"]
        incl = (r >= c) if it["lower"] else (r <= c)
        strict = (r > c) if it["lower"] else (r < c)
        gcol = jnp.where(first, cs[:, CS_G + c0:CS_G + c0 + 1], cs[:, CS_G + c0 + 1:CS_G + c0 + 2])
        decay = jnp.where(incl, jnp.exp(jnp.where(incl, gcol - it["grow"], 0.0)), 0.0)
        a = jnp.where(strict, it["kq"][:CHUNK] * decay, 0.0)
        it["intra"] = (it["kq"][CHUNK:] * decay).astype(BF16)
        it["p"] = eye - a
        it["a"] = a.astype(BF16)

    for it in items:
        it["y"] = _dot(it["a"], _block_diag(it["a"], CHUNK)).astype(BF16)
    yield
    steps = 2
    while 2 * steps < CHUNK:
        steps *= 2
        for it in items:
            res = _dot(jnp.concatenate([it["p"].astype(BF16), it["y"]], axis=0), _block_diag(it["y"], CHUNK))
            it["p"] = it["p"] + res[:CHUNK]
            it["y"] = res[CHUNK:].astype(BF16)
        yield
    for it in items:
        it["p"] = it["p"] + _dot(it["p"].astype(BF16), _block_diag(it["y"], CHUNK))
    yield

    for it in items:
        it["t"] = it["p"].astype(BF16)
        it["vb"] = it["v"] * it["beta"]
        kbg = it["kb"] * it["eg"]
        qg = it["q"] * it["eg"]
        it["kbg_qg"] = [jnp.concatenate([kbg[:, sl], qg[:, sl]], axis=0).astype(BF16)
                        for sl in (slice(0, HEAD_DIM), slice(HEAD_DIM, PAIR))]
        kg = (it["k"] * col256(it["cs"], CS_EKG, it["c0"])).astype(BF16)
        it["kgs"] = jnp.concatenate([kg[:, :HEAD_DIM], kg[:, HEAD_DIM:]], axis=0)
    yield


def _delta_scan(rounds, s_ref):
    for items in rounds:
        for it in items:
            it["state"] = s_ref[it["sidx"]]
            sb = it["state"].astype(BF16)
            ks = [_dot(it["kbg_qg"][h], sb[:, h * HEAD_DIM:(h + 1) * HEAD_DIM]) for h in range(2)]
            it["ks_top"] = jnp.concatenate([ks[0][:CHUNK], ks[1][:CHUNK]], axis=1)
            it["ws_bot"] = jnp.concatenate([ks[0][CHUNK:], ks[1][CHUNK:]], axis=1)
        yield

        for it in items:
            it["v_new"] = _dot(it["t"], _block_diag((it["vb"] - it["ks_top"]).astype(BF16), HEAD_DIM))
        yield

        for it in items:
            cs, c0 = it["cs"], it["c0"]
            v_new = _block_diag(it["v_new"].astype(BF16), HEAD_DIM)
            o_ref, bi, rows, sl = it["dst"]
            o_ref[bi, rows, sl] = it["ws_bot"] + _dot(it["intra"], v_new)
            upd = lax.dot_general(it["kgs"], v_new, (((0,), (0,)), ((), ())), preferred_element_type=F32)
            egt = jnp.concatenate([jnp.broadcast_to(cs[0:1, CS_EGT + c0:CS_EGT + c0 + 1], (1, HEAD_DIM)),
                                   jnp.broadcast_to(cs[0:1, CS_EGT + c0 + 1:CS_EGT + c0 + 2], (1, HEAD_DIM))], axis=1)
            s_ref[it["sidx"]] = it["state"] * egt + upd
        yield


def _interleave(*stage_gens):
    live = list(stage_gens)
    while live:
        for g in list(live):
            if next(g, StopIteration) is StopIteration:
                live.remove(g)


def _delta_kernel(qf_ref, kf_ref, vf_ref, csf_ref, gtf_ref, qb_ref, kb_ref, vb_ref, csb_ref, gtb_ref,
                  of_ref, ob_ref, s_ref):
    @pl.when(pl.program_id(0) == 0)
    def _():
        s_ref[...] = jnp.zeros_like(s_ref)

    nbatch = qf_ref.shape[0]
    nc = TC_DELTA // CHUNK
    npairs = N_HEADS // 2

    def chunk_items(jf):
        items = []
        for bi in range(nbatch):
            for lower, j, (q_ref, k_ref, v_ref, cs_ref, gt_ref, o_ref) in (
                    (True, jf, (qf_ref, kf_ref, vf_ref, csf_ref, gtf_ref, of_ref)),
                    (False, nc - 1 - jf, (qb_ref, kb_ref, vb_ref, csb_ref, gtb_ref, ob_ref))):
                d = 0 if lower else 1
                rows = slice(j * CHUNK, (j + 1) * CHUNK)
                cs = cs_ref[bi, rows, :]
                gt = gt_ref[bi, j]
                for hp in range(npairs):
                    sl = slice(hp * PAIR, (hp + 1) * PAIR)
                    items.append(dict(
                        q=q_ref[bi, rows, sl], k=k_ref[bi, rows, sl], v=v_ref[bi, rows, sl], cs=cs,
                        grow=gt[d * npairs + hp:d * npairs + hp + 1, :], c0=d * N_HEADS + 2 * hp,
                        sidx=(bi * 2 + d) * npairs + hp, lower=lower, dst=(o_ref, bi, rows, sl)))
        return items

    groups = [[chunk_items(g * DELTA_CHUNKS_PER_ITER + cc) for cc in range(DELTA_CHUNKS_PER_ITER)]
              for g in range(nc // DELTA_CHUNKS_PER_ITER)]
    pending = None
    for rounds in groups:
        prep = _delta_prep([it for items in rounds for it in items])
        if pending is None:
            _interleave(prep)
        else:
            _interleave(prep, _delta_scan(pending, s_ref))
        pending = rounds
    _interleave(_delta_scan(pending, s_ref))


def _delta(q, k, v, cs, gt):
    b, s, _ = q.shape
    tc = TC_DELTA
    nb = s // tc
    ncb = tc // CHUNK
    fwd = lambda w: pl.BlockSpec((b, tc, w), lambda j: (0, j, 0))
    bwd = lambda w: pl.BlockSpec((b, tc, w), lambda j: (0, nb - 1 - j, 0))
    gt_f = pl.BlockSpec((b, ncb, N_HEADS, 2 * CHUNK), lambda j: (0, j, 0, 0))
    gt_b = pl.BlockSpec((b, ncb, N_HEADS, 2 * CHUNK), lambda j: (0, nb - 1 - j, 0, 0))
    return pl.pallas_call(
        _delta_kernel,
        grid=(nb,),
        in_specs=[fwd(D_DN), fwd(D_DN), fwd(D_DN), fwd(LANES), gt_f,
                  bwd(D_DN), bwd(D_DN), bwd(D_DN), bwd(LANES), gt_b],
        out_specs=[fwd(D_DN), bwd(D_DN)],
        out_shape=[jax.ShapeDtypeStruct((b, s, D_DN), F32), jax.ShapeDtypeStruct((b, s, D_DN), F32)],
        scratch_shapes=[pltpu.VMEM((b * N_HEADS, HEAD_DIM, PAIR), F32)],
        compiler_params=pltpu.CompilerParams(dimension_semantics=("arbitrary",),
                                             vmem_limit_bytes=VMEM_LIMIT),
        name="delta",
    )(q, k, v, cs, gt, q, k, v, cs, gt)


def _out_kernel(x_ref, p_ref, of_ref, ob_ref, z_ref, ya_ref, gates_ref,
                gdn_ref, wa_ref, wdn_ref, wo_ref, gffn_ref, w1_ref, w2_ref,
                gple_ref, wpg_ref, wpp_ref, gfin_ref, out_ref):
    o = of_ref[...] + ob_ref[...]
    z = z_ref[...].astype(F32)
    gdn = gdn_ref[...]
    parts = []
    for h in range(N_HEADS):
        sl = slice(h * HEAD_DIM, (h + 1) * HEAD_DIM)
        parts.append(_rms(o[:, sl], gdn) * _silu(z[:, sl]))
    y_dn = jnp.concatenate(parts, axis=1).astype(BF16)
    gates = gates_ref[...].astype(F32)
    merged = (jax.nn.sigmoid(gates[:, :D_MODEL]) * _dot(ya_ref[...], wa_ref[...])
              + jax.nn.sigmoid(gates[:, D_MODEL:]) * _dot(y_dn, wdn_ref[...]))
    h1 = x_ref[...] + _dot(merged.astype(BF16), wo_ref[...])

    n2 = _rms(h1, gffn_ref[...]).astype(BF16)
    h2 = h1
    for c in range(D_FF // FF_CHUNK):
        sl = slice(c * FF_CHUNK, (c + 1) * FF_CHUNK)
        a = jnp.maximum(_dot(n2, w1_ref[:, sl]), 0.0)
        h2 = h2 + _dot((a * a).astype(BF16), w2_ref[sl, :])

    n3 = _rms(h2, gple_ref[...]).astype(BF16)
    gate_p = jax.nn.sigmoid(_dot(n3, wpg_ref[...]))
    h3 = h2 + gate_p * _dot(p_ref[...].astype(BF16), wpp_ref[...])
    out_ref[...] = _rms(h3, gfin_ref[...])


def _out(x2, p2, o_f, o_b, z, ya, gates, gdn, wa, wdn, wo, gffn, w1, w2, gple, wpg, wpp, gfin):
    t = x2.shape[0]
    tm = TM_OUT
    row = lambda w: pl.BlockSpec((tm, w), lambda i: (i, 0))
    weights = (gdn, wa, wdn, wo, gffn, w1, w2, gple, wpg, wpp, gfin)
    return pl.pallas_call(
        _out_kernel,
        grid=(t // tm,),
        in_specs=[row(D_MODEL), row(PLE_DIM), row(D_DN), row(D_DN), row(D_DN), row(D_CONV),
                  row(2 * D_MODEL)] + [_resident(w.shape) for w in weights],
        out_specs=row(D_MODEL),
        out_shape=jax.ShapeDtypeStruct((t, D_MODEL), F32),
        compiler_params=pltpu.CompilerParams(dimension_semantics=("arbitrary",),
                                             vmem_limit_bytes=VMEM_LIMIT),
        name="out",
    )(x2, p2, o_f, o_b, z, ya, gates, *weights)


def kernel(x, p, g_mix, w_in, conv_a, conv_qkv, a_log_f, a_log_b, dt_bias_f, dt_bias_b, g_dn_norm,
           w_a_out, w_dn_out, w_o, g_ffn, w_ff1, w_ff2, g_ple, w_ple_gate, w_ple_proj, g_final):
    b, s, d = x.shape
    depth = p.shape[0]
    assert depth == 1 and d == D_MODEL and s % TC_DELTA == 0 and s % TS_FRONT == 0
    t = b * s
    i = 0

    wt = jnp.transpose(w_in[i])

    zeros8 = jnp.zeros((2 * N_HEADS,), F32)
    tail = jnp.zeros((LANES - 4 * N_HEADS,), F32)
    alog = jnp.concatenate([zeros8, a_log_f[i], a_log_b[i], tail])[None, :]
    dtb = jnp.concatenate([zeros8, dt_bias_f[i], dt_bias_b[i], tail])[None, :]
    out_weights = (w_a_out[i], w_dn_out[i], w_o[i], w_ff1[i], w_ff2[i], w_ple_gate[i])
    ya, q, k, v, cs, gt, z, gates, *out_weights = _front(x, g_mix[i][None, :], wt,
                                                         conv_a[i], conv_qkv[i], alog, dtb, out_weights)
    wa, wdn, wo, w1, w2, wpg = out_weights
    wpp = w_ple_proj[i].astype(BF16)

    gt = gt.reshape(b, s // CHUNK, N_HEADS, 2 * CHUNK)
    o_f, o_b = _delta(q, k, v, cs, gt)

    out = _out(x.reshape(t, d), p[i].reshape(t, PLE_DIM), o_f.reshape(t, D_DN), o_b.reshape(t, D_DN),
               z.reshape(t, D_DN), ya.reshape(t, D_CONV), gates.reshape(t, 2 * D_MODEL),
               g_dn_norm[i][None, :], wa, wdn, wo, g_ffn[i][None, :], w1, w2,
               g_ple[i][None, :], wpg, wpp, g_final[None, :])
    return out.reshape(b, s, d)
```
